```python
import jax, jax.numpy as jnp
from jax import lax
import numpy as np

D_MODEL = 4096
BATCH = 2
SEQ = 8192
DEPTH = 2

EPS = 1e-6
SG_CHUNK = 128
SG_HEAD_DIM = 128
SG_WIDTH = D_MODEL
SG_HEADS = SG_WIDTH // SG_HEAD_DIM
GLA_HEADS = 4
GLA_KEY_DIM = D_MODEL // 2
GLA_VALUE_DIM = D_MODEL
GLA_DK = GLA_KEY_DIM // GLA_HEADS
GLA_DV = GLA_VALUE_DIM // GLA_HEADS
GLA_CHUNK = 64
GLA_GATE_RANK = 16
GLA_GATE_NORM = 16.0
GLA_IN = 2 * GLA_KEY_DIM + 2 * GLA_VALUE_DIM + GLA_GATE_RANK
D_FF = 7 * D_MODEL // 2
N_EXPERTS = 8
TOP_K = 2
D_FF_EXPERT = D_FF // 4
PLE_DIM = 256

kernel_name = "hybrid_gmlp_gla_moe_trunk"


def rmsnorm(x, g):
    xf = x.astype(jnp.float32)
    y = xf * lax.rsqrt(jnp.mean(xf * xf, axis=-1, keepdims=True) + EPS)
    return (y * g.astype(jnp.float32)).astype(x.dtype)


def swiglu(h, w_gu, w_down):
    gate, up = jnp.split(h @ w_gu, 2, axis=-1)
    return (jax.nn.silu(gate) * up) @ w_down


def spatial_gating_mixer(h, w_in, g_v, w_s, b_s, w_out):
    bsz, seq, _ = h.shape
    z = jax.nn.gelu(h @ w_in, approximate=False)
    u, v = jnp.split(z, 2, axis=-1)
    v = rmsnorm(v, g_v).reshape(bsz, seq // SG_CHUNK, SG_CHUNK, SG_HEADS, SG_HEAD_DIM)
    causal = jnp.tril(jnp.ones((SG_CHUNK, SG_CHUNK), dtype=bool))
    w_causal = jnp.where(causal[None], w_s, jnp.zeros_like(w_s))
    mixed = jnp.einsum('hts,bcshd->bcthd', w_causal, v) + b_s.T[None, None, :, :, None]
    return (u * mixed.reshape(bsz, seq, SG_WIDTH)) @ w_out


def gla_mixer(h, w_in, w_alpha, b_alpha, g_o, w_out):
    bsz, seq, _ = h.shape
    splits = [GLA_KEY_DIM, 2 * GLA_KEY_DIM, 2 * GLA_KEY_DIM + GLA_VALUE_DIM,
              2 * GLA_KEY_DIM + 2 * GLA_VALUE_DIM]
    q, k, v, r, a_low = jnp.split(h @ w_in, splits, axis=-1)
    log_alpha = jax.nn.log_sigmoid((a_low @ w_alpha + b_alpha).astype(jnp.float32)) / GLA_GATE_NORM
    n_chunks = seq // GLA_CHUNK

    def to_chunks(t, d):
        return t.astype(jnp.float32).reshape(bsz, n_chunks, GLA_CHUNK, GLA_HEADS, d).transpose(1, 0, 3, 2, 4)

    qc = to_chunks(q, GLA_DK) * (GLA_DK ** -0.5)
    kc = to_chunks(k, GLA_DK)
    vc = to_chunks(v, GLA_DV)
    gc = to_chunks(log_alpha, GLA_DK)
    causal = jnp.tril(jnp.ones((GLA_CHUNK, GLA_CHUNK), dtype=bool))[:, :, None]

    def chunk_step(state, inp):
        q_, k_, v_, g_ = inp
        b = jnp.cumsum(g_, axis=2)
        o_inter = jnp.einsum('bhtk,bhkv->bhtv', q_ * jnp.exp(b), state)
        diff = b[:, :, :, None, :] - b[:, :, None, :, :]
        decay = jnp.exp(jnp.where(causal, diff, -jnp.inf))
        scores = jnp.einsum('bhtk,bhsk,bhtsk->bhts', q_, k_, decay)
        o = o_inter + jnp.einsum('bhts,bhsv->bhtv', scores, v_)
        b_last = b[:, :, -1:, :]
        k_dec = k_ * jnp.exp(b_last - b)
        new_state = jnp.exp(b_last[:, :, 0, :])[..., None] * state + jnp.einsum('bhsk,bhsv->bhkv', k_dec, v_)
        return new_state, o

    state0 = jnp.zeros((bsz, GLA_HEADS, GLA_DK, GLA_DV), jnp.float32)
    _, o = lax.scan(chunk_step, state0, (qc, kc, vc, gc))
    o = o.transpose(1, 0, 3, 2, 4).reshape(bsz, seq, GLA_HEADS, GLA_DV)
    o = rmsnorm(o, g_o.reshape(GLA_HEADS, GLA_DV)).reshape(bsz, seq, GLA_VALUE_DIM).astype(h.dtype)
    return (o * jax.nn.silu(r)) @ w_out


def moe_swiglu(h, w_router, w_gu, w_down):
    logits = (h @ w_router).astype(jnp.float32)
    top_val, top_idx = lax.top_k(logits, TOP_K)
    top_w = jax.nn.softmax(top_val, axis=-1)
    gates = jnp.sum(jax.nn.one_hot(top_idx, N_EXPERTS, dtype=jnp.float32) * top_w[..., None], axis=-2)
    gates = gates.astype(h.dtype)
    out = jnp.zeros_like(h)
    for e in range(N_EXPERTS):
        out = out + gates[..., e:e + 1] * swiglu(h, w_gu[e], w_down[e])
    return out


def per_layer_embedding(x, p_i, g_norm, w_gate, w_proj):
    gate = jax.nn.sigmoid(rmsnorm(x, g_norm) @ w_gate)
    return gate * (p_i @ w_proj)


def setup_inputs(seed: int = 0) -> dict:
    key = jax.random.key(seed)
    keys = iter(jax.random.split(key, 40))
    n_even = (DEPTH + 1) // 2
    n_odd = DEPTH // 2

    def normal(shape, scale):
        return scale * jax.random.normal(next(keys), shape, jnp.float32)

    def gain(shape):
        return 1.0 + 0.01 * jax.random.normal(next(keys), shape, jnp.float32)

    return {
        'x': normal((BATCH, SEQ, D_MODEL), 1.0),
        'p': normal((DEPTH, BATCH, SEQ, PLE_DIM), 1.0),
        'mix_norm': gain((DEPTH, D_MODEL)),
        'ffn_norm': gain((DEPTH, D_MODEL)),
        'ple_norm': gain((DEPTH, D_MODEL)),
        'ple_w_gate': normal((DEPTH, D_MODEL, D_MODEL), D_MODEL ** -0.5),
        'ple_w_proj': normal((DEPTH, PLE_DIM, D_MODEL), PLE_DIM ** -0.5),
        'sg_w_in': normal((n_even, D_MODEL, 2 * SG_WIDTH), D_MODEL ** -0.5),
        'sg_g_v': gain((n_even, SG_WIDTH)),
        'sg_w_s': normal((n_even, SG_HEADS, SG_CHUNK, SG_CHUNK), SG_CHUNK ** -0.5),
        'sg_b_s': 1.0 + normal((n_even, SG_HEADS, SG_CHUNK), 0.1),
        'sg_w_out': normal((n_even, SG_WIDTH, D_MODEL), SG_WIDTH ** -0.5),
        'ffn_w_gu': normal((n_even, D_MODEL, 2 * D_FF), D_MODEL ** -0.5),
        'ffn_w_down': normal((n_even, D_FF, D_MODEL), D_FF ** -0.5),
        'gla_w_in': normal((n_odd, D_MODEL, GLA_IN), D_MODEL ** -0.5),
        'gla_w_alpha': normal((n_odd, GLA_GATE_RANK, GLA_KEY_DIM), GLA_GATE_RANK ** -0.5),
        'gla_b_alpha': normal((n_odd, GLA_KEY_DIM), 0.1),
        'gla_g_o': gain((n_odd, GLA_VALUE_DIM)),
        'gla_w_out': normal((n_odd, GLA_VALUE_DIM, D_MODEL), GLA_VALUE_DIM ** -0.5),
        'moe_w_router': normal((n_odd, D_MODEL, N_EXPERTS), D_MODEL ** -0.5),
        'moe_w_gu': normal((n_odd, N_EXPERTS, D_MODEL, 2 * D_FF_EXPERT), D_MODEL ** -0.5),
        'moe_w_down': normal((n_odd, N_EXPERTS, D_FF_EXPERT, D_MODEL), D_FF_EXPERT ** -0.5),
        'final_norm': gain((D_MODEL,)),
    }


def reference(x, p, mix_norm, ffn_norm, ple_norm, ple_w_gate, ple_w_proj,
              sg_w_in, sg_g_v, sg_w_s, sg_b_s, sg_w_out, ffn_w_gu, ffn_w_down,
              gla_w_in, gla_w_alpha, gla_b_alpha, gla_g_o, gla_w_out,
              moe_w_router, moe_w_gu, moe_w_down, final_norm):
    for i in range(DEPTH):
        j = i // 2
        h = rmsnorm(x, mix_norm[i])
        if i % 2 == 0:
            x = x + spatial_gating_mixer(h, sg_w_in[j], sg_g_v[j], sg_w_s[j], sg_b_s[j], sg_w_out[j])
        else:
            x = x + gla_mixer(h, gla_w_in[j], gla_w_alpha[j], gla_b_alpha[j], gla_g_o[j], gla_w_out[j])
        h = rmsnorm(x, ffn_norm[i])
        if i % 2 == 0:
            x = x + swiglu(h, ffn_w_gu[j], ffn_w_down[j])
        else:
            x = x + moe_swiglu(h, moe_w_router[j], moe_w_gu[j], moe_w_down[j])
        x = x + per_layer_embedding(x, p[i], ple_norm[i], ple_w_gate[i], ple_w_proj[i])
    return rmsnorm(x, final_norm)
```

```python
import functools

import jax
import jax.numpy as jnp
import numpy as np
from jax import lax
from jax.experimental import pallas as pl
from jax.experimental.pallas import tpu as pltpu

EPS = 1e-6
SG_CHUNK = 128
SG_HEAD_DIM = 128
GLA_HEADS = 4
GLA_GATE_RANK = 16
GLA_GATE_NORM = 16.0
TOP_K = 2

V7X_VMEM_LIMIT_BYTES = 56 * 1024 * 1024
LANES = 128
GLA_KERNEL_CHUNK = 128
GLA_SUB = 16
MOE_ROW_TILE = 512

F32 = jnp.float32
BF16 = jnp.bfloat16


def _tile(dim, pref, align=LANES):
    if dim <= pref:
        return dim
    t = (pref // align) * align
    while t > align and dim % t:
        t -= align
    assert dim % t == 0, (dim, pref)
    return t


def _params(*sem):
    return pltpu.CompilerParams(dimension_semantics=sem, vmem_limit_bytes=V7X_VMEM_LIMIT_BYTES)


def _rmsnorm(x, g):
    ms = jnp.mean(x * x, axis=-1, keepdims=True)
    return x * lax.rsqrt(ms + EPS) * g


def _sigmoid(x):
    return 1.0 / (1.0 + jnp.exp(-x))


def _silu(x):
    return x * _sigmoid(x)


def _dot(a, b):
    return jnp.dot(a, b, preferred_element_type=F32)


def _rmsnorm_kernel(x_ref, g_ref, o_ref):
    o_ref[...] = _rmsnorm(x_ref[...], g_ref[...]).astype(o_ref.dtype)


def rmsnorm(x, g, out_dtype):
    n, d = x.shape
    tm = _tile(n, 512, 8)
    return pl.pallas_call(
        _rmsnorm_kernel,
        grid=(n // tm,),
        in_specs=[pl.BlockSpec((tm, d), lambda i: (i, 0)), pl.BlockSpec((1, d), lambda i: (0, 0))],
        out_specs=pl.BlockSpec((tm, d), lambda i: (i, 0)),
        out_shape=jax.ShapeDtypeStruct((n, d), out_dtype),
        compiler_params=_params("parallel"),
        name="rmsnorm",
    )(x, g.reshape(1, d))


def _mm_kernel(x_ref, w_ref, o_ref, *, act):
    acc = _dot(x_ref[...], w_ref[...])
    if act == "gelu":
        acc = 0.5 * acc * (1.0 + lax.erf(acc * np.float32(np.sqrt(0.5))))
    o_ref[...] = acc.astype(o_ref.dtype)


def matmul(x, w, act=None, out_dtype=BF16, tm=1024, tn=1024):
    n, k = x.shape
    m = w.shape[1]
    tm, tn = _tile(n, tm, 8), _tile(m, tn)
    return pl.pallas_call(
        functools.partial(_mm_kernel, act=act),
        grid=(n // tm, m // tn),
        in_specs=[pl.BlockSpec((tm, k), lambda i, j: (i, 0)), pl.BlockSpec((k, tn), lambda i, j: (0, j))],
        out_specs=pl.BlockSpec((tm, tn), lambda i, j: (i, j)),
        out_shape=jax.ShapeDtypeStruct((n, m), out_dtype),
        compiler_params=_params("parallel", "parallel"),
        name="matmul_" + str(act),
    )(x, w)


def _mm_res_kernel(x_ref, w_ref, r_ref, o_ref):
    k = pl.program_id(2)

    @pl.when(k == 0)
    def _():
        o_ref[...] = r_ref[...]

    o_ref[...] += _dot(x_ref[...], w_ref[...])


def matmul_residual(x, w, res, tm=1024, tn=1024, tk=4096):
    n, k = x.shape
    m = w.shape[1]
    tm, tn, tk = _tile(n, tm, 8), _tile(m, tn), _tile(k, tk)
    return pl.pallas_call(
        _mm_res_kernel,
        grid=(n // tm, m // tn, k // tk),
        in_specs=[pl.BlockSpec((tm, tk), lambda i, j, kk: (i, kk)),
                  pl.BlockSpec((tk, tn), lambda i, j, kk: (kk, j)),
                  pl.BlockSpec((tm, tn), lambda i, j, kk: (i, j))],
        out_specs=pl.BlockSpec((tm, tn), lambda i, j, kk: (i, j)),
        out_shape=jax.ShapeDtypeStruct((n, m), F32),
        compiler_params=_params("parallel", "parallel", "arbitrary"),
        name="matmul_residual",
    )(x, w, res)


def _swiglu_kernel(x_ref, wg_ref, wu_ref, o_ref):
    x = x_ref[...]
    o_ref[...] = (_silu(_dot(x, wg_ref[...])) * _dot(x, wu_ref[...])).astype(o_ref.dtype)


def swiglu_up(x, w_gu, tm=1024, tn=512):
    n, k = x.shape
    f = w_gu.shape[1] // 2
    tm, tn = _tile(n, tm, 8), _tile(f, tn)
    nj = f // tn
    return pl.pallas_call(
        _swiglu_kernel,
        grid=(n // tm, nj),
        in_specs=[pl.BlockSpec((tm, k), lambda i, j: (i, 0)),
                  pl.BlockSpec((k, tn), lambda i, j: (0, j)),
                  pl.BlockSpec((k, tn), lambda i, j: (0, j + nj))],
        out_specs=pl.BlockSpec((tm, tn), lambda i, j: (i, j)),
        out_shape=jax.ShapeDtypeStruct((n, f), BF16),
        compiler_params=_params("parallel", "parallel"),
        name="swiglu_up",
    )(x, w_gu, w_gu)


def _ple_kernel(h_ref, wg_ref, p_ref, wp_ref, x_ref, o_ref):
    gate = _sigmoid(_dot(h_ref[...], wg_ref[...]))
    o_ref[...] = x_ref[...] + gate * _dot(p_ref[...], wp_ref[...])


def ple(h, w_gate, p, w_proj, x, tm=512, tn=1024):
    n, d = h.shape
    pd = p.shape[1]
    m = w_gate.shape[1]
    tm, tn = _tile(n, tm, 8), _tile(m, tn)
    return pl.pallas_call(
        _ple_kernel,
        grid=(n // tm, m // tn),
        in_specs=[pl.BlockSpec((tm, d), lambda i, j: (i, 0)),
                  pl.BlockSpec((d, tn), lambda i, j: (0, j)),
                  pl.BlockSpec((tm, pd), lambda i, j: (i, 0)),
                  pl.BlockSpec((pd, tn), lambda i, j: (0, j)),
                  pl.BlockSpec((tm, tn), lambda i, j: (i, j))],
        out_specs=pl.BlockSpec((tm, tn), lambda i, j: (i, j)),
        out_shape=jax.ShapeDtypeStruct((n, m), F32),
        compiler_params=_params("parallel", "parallel"),
        name="ple",
    )(h, w_gate, p, w_proj, x)


def _sg_kernel(u_ref, v_ref, gv_ref, ws_ref, bias_ref, o_ref, *, heads):
    tb = u_ref.shape[0]
    t, hd = SG_CHUNK, SG_HEAD_DIM
    vn = _rmsnorm(v_ref[...].astype(F32), gv_ref[...]).astype(BF16)
    causal = lax.broadcasted_iota(jnp.int32, (t, t), 0) >= lax.broadcasted_iota(jnp.int32, (t, t), 1)
    for h in range(heads):
        cols = slice(h * hd, (h + 1) * hd)
        wc = jnp.where(causal, ws_ref[h], 0.0).astype(BF16)
        bias = bias_ref[:, cols]
        for c in range(tb // t):
            rows = slice(c * t, (c + 1) * t)
            mixed = _dot(wc, vn[rows, cols]) + bias
            o_ref[rows, cols] = (u_ref[rows, cols].astype(F32) * mixed).astype(o_ref.dtype)


def spatial_gate(z, g_v, w_s, b_s, tb=512):
    n, w2 = z.shape
    w = w2 // 2
    heads = w // SG_HEAD_DIM
    tb = _tile(n, tb, SG_CHUNK)
    bias = jnp.repeat(b_s.T, SG_HEAD_DIM, axis=1)
    return pl.pallas_call(
        functools.partial(_sg_kernel, heads=heads),
        grid=(n // tb,),
        in_specs=[pl.BlockSpec((tb, w), lambda i: (i, 0)),
                  pl.BlockSpec((tb, w), lambda i: (i, 1)),
                  pl.BlockSpec((1, w), lambda i: (0, 0)),
                  pl.BlockSpec((heads, SG_CHUNK, SG_CHUNK), lambda i: (0, 0, 0)),
                  pl.BlockSpec((SG_CHUNK, w), lambda i: (0, 0))],
        out_specs=pl.BlockSpec((tb, w), lambda i: (i, 0)),
        out_shape=jax.ShapeDtypeStruct((n, w), BF16),
        compiler_params=_params("parallel"),
        name="spatial_gate",
    )(z, z, g_v.reshape(1, w), w_s, bias)


def _gla_gate_kernel(h_ref, wa_ref, walpha_ref, b_ref, o_ref):
    a_low = _dot(h_ref[...], wa_ref[...])
    z = jnp.dot(a_low, walpha_ref[...], precision=lax.Precision.HIGHEST, preferred_element_type=F32) + b_ref[...]
    o_ref[...] = (jnp.minimum(z, 0.0) - jnp.log1p(jnp.exp(-jnp.abs(z)))) * np.float32(1.0 / GLA_GATE_NORM)


def gla_log_alpha(h, w_a, w_alpha, b_alpha, tm=1024):
    n, d = h.shape
    rank, kd = w_alpha.shape
    tm = _tile(n, tm, 8)
    w_a = jnp.pad(w_a, ((0, 0), (0, LANES - rank)))
    w_alpha = jnp.pad(w_alpha, ((0, LANES - rank), (0, 0)))
    return pl.pallas_call(
        _gla_gate_kernel,
        grid=(n // tm,),
        in_specs=[pl.BlockSpec((tm, d), lambda i: (i, 0)),
                  pl.BlockSpec((d, LANES), lambda i: (0, 0)),
                  pl.BlockSpec((LANES, kd), lambda i: (0, 0)),
                  pl.BlockSpec((1, kd), lambda i: (0, 0))],
        out_specs=pl.BlockSpec((tm, kd), lambda i: (i, 0)),
        out_shape=jax.ShapeDtypeStruct((n, kd), F32),
        compiler_params=_params("parallel"),
        name="gla_log_alpha",
    )(h, w_a, w_alpha, b_alpha.reshape(1, kd))


def _gla_kernel(q_ref, k_ref, v_ref, r_ref, g_ref, go_ref, o_ref, s_ref, a_ref, *, scale):
    c, dk = q_ref.shape
    sub = GLA_SUB

    @pl.when(pl.program_id(2) == 0)
    def _():
        s_ref[...] = jnp.zeros_like(s_ref)

    q = q_ref[...].astype(F32) * scale
    k = k_ref[...].astype(F32)
    v = v_ref[...]
    tri = (lax.broadcasted_iota(jnp.int32, (c, c), 0) >= lax.broadcasted_iota(jnp.int32, (c, c), 1)).astype(F32)
    b = jnp.dot(tri, g_ref[...], precision=lax.Precision.HIGHEST, preferred_element_type=F32)

    o = _dot((q * jnp.exp(b)).astype(BF16), s_ref[...].astype(BF16))

    lane = lax.broadcasted_iota(jnp.int32, (sub, c), 1)
    t_loc = lax.broadcasted_iota(jnp.int32, (sub, 1), 0)
    for i in range(c // sub):
        i0 = i * sub
        qi = q[i0:i0 + sub]
        bi = b[i0:i0 + sub]
        ki = k[i0:i0 + sub]
        a_rows = jnp.zeros((sub, c), F32)
        if i > 0:
            ref = b[i0 - 1:i0]
            qt = (qi * jnp.exp(bi - ref)).astype(BF16)
            kt = (k * jnp.exp(jnp.minimum(ref - b, 0.0))).astype(BF16)
            a_off = lax.dot_general(qt, kt, (((1,), (1,)), ((), ())), preferred_element_type=F32)
            a_rows = jnp.where(lane < i0, a_off, 0.0)
        for s in range(sub):
            e = jnp.exp(jnp.where(t_loc >= s, bi - bi[s:s + 1], -jnp.inf))
            col = jnp.sum(qi * ki[s:s + 1] * e, axis=-1, keepdims=True)
            a_rows = a_rows + jnp.where(lane == i0 + s, col, 0.0)
        a_ref[i0:i0 + sub, :] = a_rows
    o = o + _dot(a_ref[...].astype(BF16), v)

    b_last = b[c - 1:c]
    k_dec = (k * jnp.exp(b_last - b)).astype(BF16)
    upd = lax.dot_general(k_dec, v, (((0,), (0,)), ((), ())), preferred_element_type=F32)
    decay_col = jnp.transpose(jnp.broadcast_to(jnp.exp(b_last), (LANES, dk)))[:, 0:1]
    s_ref[...] = decay_col * s_ref[...] + upd

    r = r_ref[...].astype(F32)
    o_ref[...] = (_rmsnorm(o, go_ref[...]) * _silu(r)).astype(o_ref.dtype)


def gla(qkvr, log_alpha, g_o, bsz, seq):
    n, kd = log_alpha.shape
    vd = (qkvr.shape[1] - 2 * kd) // 2
    dk, dv = kd // GLA_HEADS, vd // GLA_HEADS
    c = _tile(seq, GLA_KERNEL_CHUNK, GLA_SUB)
    nc = seq // c
    assert kd % dk == 0 and (2 * kd) % dv == 0 and (2 * kd + vd) % dv == 0
    k_blk, v_blk, r_blk = kd // dk, (2 * kd) // dv, (2 * kd + vd) // dv

    def row(b, h, j):
        return b * nc + j

    return pl.pallas_call(
        functools.partial(_gla_kernel, scale=np.float32(dk ** -0.5)),
        grid=(bsz, GLA_HEADS, nc),
        in_specs=[pl.BlockSpec((c, dk), lambda b, h, j: (row(b, h, j), h)),
                  pl.BlockSpec((c, dk), lambda b, h, j: (row(b, h, j), k_blk + h)),
                  pl.BlockSpec((c, dv), lambda b, h, j: (row(b, h, j), v_blk + h)),
                  pl.BlockSpec((c, dv), lambda b, h, j: (row(b, h, j), r_blk + h)),
                  pl.BlockSpec((c, dk), lambda b, h, j: (row(b, h, j), h)),
                  pl.BlockSpec((1, dv), lambda b, h, j: (0, h))],
        out_specs=pl.BlockSpec((c, dv), lambda b, h, j: (row(b, h, j), h)),
        out_shape=jax.ShapeDtypeStruct((n, vd), BF16),
        scratch_shapes=[pltpu.VMEM((dk, dv), F32), pltpu.VMEM((c, c), F32)],
        compiler_params=_params("parallel", "parallel", "arbitrary"),
        name="gla",
    )(qkvr, qkvr, qkvr, qkvr, log_alpha, g_o.reshape(1, vd))


def _router_kernel(x_ref, g_ref, wr_ref, h_ref, gates_ref, sel_ref, *, n_experts):
    h = _rmsnorm(x_ref[...], g_ref[...])
    h_ref[...] = h
    logits = jnp.dot(h, wr_ref[...], precision=lax.Precision.HIGHEST, preferred_element_type=F32)
    lane = lax.broadcasted_iota(jnp.int32, logits.shape, 1)
    neg = -jnp.inf
    logits = jnp.where(lane < n_experts, logits, neg)
    m1 = jnp.max(logits, axis=-1, keepdims=True)
    i1 = jnp.min(jnp.where(logits == m1, lane, LANES), axis=-1, keepdims=True)
    is1 = lane == i1
    rest = jnp.where(is1, neg, logits)
    m2 = jnp.max(rest, axis=-1, keepdims=True)
    i2 = jnp.min(jnp.where(rest == m2, lane, LANES), axis=-1, keepdims=True)
    is2 = lane == i2
    e2 = jnp.exp(m2 - m1)
    den = 1.0 + e2
    gates_ref[...] = jnp.where(is1, 1.0 / den, 0.0) + jnp.where(is2, e2 / den, 0.0)
    sel_ref[...] = jnp.where(is1 | is2, 1, 0).astype(jnp.int32)


def moe_router(x, g, w_router, tm=256):
    n, d = x.shape
    e = w_router.shape[1]
    tm = _tile(n, tm, 8)
    wr = jnp.pad(w_router, ((0, 0), (0, LANES - e)))
    return pl.pallas_call(
        functools.partial(_router_kernel, n_experts=e),
        grid=(n // tm,),
        in_specs=[pl.BlockSpec((tm, d), lambda i: (i, 0)),
                  pl.BlockSpec((1, d), lambda i: (0, 0)),
                  pl.BlockSpec((d, LANES), lambda i: (0, 0))],
        out_specs=[pl.BlockSpec((tm, d), lambda i: (i, 0)),
                   pl.BlockSpec((tm, LANES), lambda i: (i, 0)),
                   pl.BlockSpec((tm, LANES), lambda i: (i, 0))],
        out_shape=[jax.ShapeDtypeStruct((n, d), F32),
                   jax.ShapeDtypeStruct((n, LANES), F32),
                   jax.ShapeDtypeStruct((n, LANES), jnp.int32)],
        compiler_params=_params("parallel"),
        name="moe_router",
    )(x, g.reshape(1, d), wr)


def _row_copy(src_ref, src_row, dst_ref, dst_row, sem):
    return pltpu.make_async_copy(src_ref.at[pl.ds(src_row, 1)], dst_ref.at[pl.ds(dst_row, 1)], sem)


def _dispatch_kernel(dest_ref, h_ref, xs_in_ref, xs_ref, sem):
    del xs_in_ref
    tm = h_ref.shape[0]

    def start(r, carry):
        for kk in range(TOP_K):
            _row_copy(h_ref, r, xs_ref, dest_ref[0, 0, TOP_K * r + kk], sem).start()
        return carry

    lax.fori_loop(0, tm, start, 0)

    def wait(r, carry):
        for kk in range(TOP_K):
            _row_copy(h_ref, 0, xs_ref, 0, sem).wait()
        return carry

    lax.fori_loop(0, tm, wait, 0)


def moe_dispatch(h, dest, n_rows, tm=256):
    n, d = h.shape
    tm = _tile(n, tm, 8)
    dest3 = dest.reshape(n // tm, 1, TOP_K * tm)
    zeros = jnp.zeros((n_rows, d), h.dtype)
    return pl.pallas_call(
        _dispatch_kernel,
        grid=(n // tm,),
        in_specs=[pl.BlockSpec((1, 1, TOP_K * tm), lambda i: (i, 0, 0), memory_space=pltpu.SMEM),
                  pl.BlockSpec((tm, d), lambda i: (i, 0)),
                  pl.BlockSpec(memory_space=pl.ANY)],
        out_specs=pl.BlockSpec(memory_space=pl.ANY),
        out_shape=jax.ShapeDtypeStruct((n_rows, d), h.dtype),
        scratch_shapes=[pltpu.SemaphoreType.DMA(())],
        input_output_aliases={2: 0},
        compiler_params=_params("arbitrary"),
        name="moe_dispatch",
    )(dest3, h, zeros)


def _moe_up_kernel(te_ref, nt_ref, x_ref, wg_ref, wu_ref, o_ref, xb_ref):
    i, j = pl.program_id(0), pl.program_id(1)

    @pl.when(i < nt_ref[0])
    def _():
        @pl.when(j == 0)
        def _():
            xb_ref[...] = x_ref[...].astype(BF16)

        x = xb_ref[...]
        o_ref[...] = (_silu(_dot(x, wg_ref[0])) * _dot(x, wu_ref[0])).astype(o_ref.dtype)

    @pl.when(i >= nt_ref[0])
    def _():
        o_ref[...] = jnp.zeros_like(o_ref)


def moe_up(xs, w_gu, tile_expert, n_tiles, tn=512):
    rows, d = xs.shape
    f = w_gu.shape[2] // 2
    tm, tn = MOE_ROW_TILE, _tile(f, tn)
    nj = f // tn
    grid_spec = pltpu.PrefetchScalarGridSpec(
        num_scalar_prefetch=2,
        grid=(rows // tm, nj),
        in_specs=[pl.BlockSpec((tm, d), lambda i, j, te, nt: (i, 0)),
                  pl.BlockSpec((1, d, tn), lambda i, j, te, nt: (te[i], 0, j)),
                  pl.BlockSpec((1, d, tn), lambda i, j, te, nt: (te[i], 0, j + nj))],
        out_specs=pl.BlockSpec((tm, tn), lambda i, j, te, nt: (i, j)),
        scratch_shapes=[pltpu.VMEM((tm, d), BF16)],
    )
    return pl.pallas_call(
        _moe_up_kernel,
        grid_spec=grid_spec,
        out_shape=jax.ShapeDtypeStruct((rows, f), BF16),
        compiler_params=_params("parallel", "arbitrary"),
        name="moe_up",
    )(tile_expert, n_tiles, xs, w_gu, w_gu)


def _moe_down_kernel(te_ref, nt_ref, a_ref, w_ref, o_ref):
    i = pl.program_id(0)

    @pl.when(i < nt_ref[0])
    def _():
        o_ref[...] = _dot(a_ref[...], w_ref[0])

    @pl.when(i >= nt_ref[0])
    def _():
        o_ref[...] = jnp.zeros_like(o_ref)


def moe_down(a, w_down, tile_expert, n_tiles, tn=1024):
    rows, f = a.shape
    d = w_down.shape[2]
    tm, tn = MOE_ROW_TILE, _tile(d, tn)
    grid_spec = pltpu.PrefetchScalarGridSpec(
        num_scalar_prefetch=2,
        grid=(rows // tm, d // tn),
        in_specs=[pl.BlockSpec((tm, f), lambda i, j, te, nt: (i, 0)),
                  pl.BlockSpec((1, f, tn), lambda i, j, te, nt: (te[i], 0, j))],
        out_specs=pl.BlockSpec((tm, tn), lambda i, j, te, nt: (i, j)),
    )
    return pl.pallas_call(
        _moe_down_kernel,
        grid_spec=grid_spec,
        out_shape=jax.ShapeDtypeStruct((rows, d), F32),
        compiler_params=_params("parallel", "parallel"),
        name="moe_down",
    )(tile_expert, n_tiles, a, w_down)


def _combine_kernel(dest_ref, x_ref, w_ref, y_ref, o_ref, buf_ref, sem):
    tm = x_ref.shape[0]

    def start(r, carry):
        for kk in range(TOP_K):
            _row_copy(y_ref, dest_ref[0, 0, TOP_K * r + kk], buf_ref.at[kk], r, sem).start()
        return carry

    lax.fori_loop(0, tm, start, 0)

    def wait(r, carry):
        for kk in range(TOP_K):
            _row_copy(y_ref, 0, buf_ref.at[kk], 0, sem).wait()
        return carry

    lax.fori_loop(0, tm, wait, 0)
    w = w_ref[...]
    o_ref[...] = x_ref[...] + (w[:, 0:1] * buf_ref[0] + w[:, 1:2] * buf_ref[1])


def moe_combine(x, y, dest, w, tm=256):
    n, d = x.shape
    tm = _tile(n, tm, 8)
    dest3 = dest.reshape(n // tm, 1, TOP_K * tm)
    return pl.pallas_call(
        _combine_kernel,
        grid=(n // tm,),
        in_specs=[pl.BlockSpec((1, 1, TOP_K * tm), lambda i: (i, 0, 0), memory_space=pltpu.SMEM),
                  pl.BlockSpec((tm, d), lambda i: (i, 0)),
                  pl.BlockSpec((tm, LANES), lambda i: (i, 0)),
                  pl.BlockSpec(memory_space=pl.ANY)],
        out_specs=pl.BlockSpec((tm, d), lambda i: (i, 0)),
        out_shape=jax.ShapeDtypeStruct((n, d), F32),
        scratch_shapes=[pltpu.VMEM((TOP_K, tm, d), F32), pltpu.SemaphoreType.DMA(())],
        compiler_params=_params("arbitrary"),
        name="moe_combine",
    )(dest3, x, w, y)


def moe_layer(x, g, w_router, w_gu, w_down):
    n, d = x.shape
    n_exp = w_router.shape[1]
    tm = MOE_ROW_TILE
    h, gates, sel = moe_router(x, g, w_router)
    gates, sel = gates[:, :n_exp], sel[:, :n_exp]

    csum = jnp.cumsum(sel, axis=0)
    counts = csum[-1]
    group = ((counts + tm - 1) // tm) * tm
    ends = jnp.cumsum(group)
    slot = (ends - group)[None, :] + csum - 1
    big = jnp.iinfo(jnp.int32).max
    d_lo = jnp.min(jnp.where(sel > 0, slot, big), axis=1)
    d_hi = jnp.max(jnp.where(sel > 0, slot, -1), axis=1)
    w_lo = jnp.sum(jnp.where((sel > 0) & (slot == d_lo[:, None]), gates, 0.0), axis=1)
    w_hi = jnp.sum(jnp.where((sel > 0) & (slot == d_hi[:, None]), gates, 0.0), axis=1)
    dest = jnp.stack([d_lo, d_hi], axis=1).astype(jnp.int32)
    w = jnp.pad(jnp.stack([w_lo, w_hi], axis=1), ((0, 0), (0, LANES - TOP_K)))

    n_rows = TOP_K * n + n_exp * tm
    n_row_tiles = n_rows // tm
    tile_start = jnp.arange(n_row_tiles, dtype=jnp.int32) * tm
    tile_expert = jnp.minimum(jnp.searchsorted(ends, tile_start, side="right"), n_exp - 1).astype(jnp.int32)
    n_tiles = (ends[-1:] // tm).astype(jnp.int32)

    xs = moe_dispatch(h, dest, n_rows)
    a = moe_up(xs, w_gu, tile_expert, n_tiles)
    y = moe_down(a, w_down, tile_expert, n_tiles)
    return moe_combine(x, y, dest, w)


def kernel(x, p, mix_norm, ffn_norm, ple_norm, ple_w_gate, ple_w_proj, sg_w_in, sg_g_v, sg_w_s, sg_b_s, sg_w_out,
           ffn_w_gu, ffn_w_down, gla_w_in, gla_w_alpha, gla_b_alpha, gla_g_o, gla_w_out, moe_w_router, moe_w_gu,
           moe_w_down, final_norm):
    bsz, seq, d = x.shape
    n = bsz * seq
    depth = p.shape[0]
    x = x.reshape(n, d)
    p = p.reshape(depth, n, -1).astype(BF16)
    bf = lambda w: w.astype(BF16)

    for i in range(depth):
        j = i // 2
        h = rmsnorm(x, mix_norm[i], BF16)
        if i % 2 == 0:
            z = matmul(h, bf(sg_w_in[j]), act="gelu")
            gated = spatial_gate(z, sg_g_v[j], sg_w_s[j], sg_b_s[j])
            x = matmul_residual(gated, bf(sg_w_out[j]), x)
            h = rmsnorm(x, ffn_norm[i], BF16)
            a = swiglu_up(h, bf(ffn_w_gu[j]))
            x = matmul_residual(a, bf(ffn_w_down[j]), x, tk=2048)
        else:
            kd = gla_w_alpha.shape[2]
            n_main = gla_w_in.shape[2] - GLA_GATE_RANK
            qkvr = matmul(h, bf(gla_w_in[j, :, :n_main]))
            log_alpha = gla_log_alpha(h, bf(gla_w_in[j, :, n_main:]), gla_w_alpha[j], gla_b_alpha[j])
            og = gla(qkvr, log_alpha, gla_g_o[j], bsz, seq)
            x = matmul_residual(og, bf(gla_w_out[j]), x)
            x = moe_layer(x, ffn_norm[i], moe_w_router[j], bf(moe_w_gu[j]), bf(moe_w_down[j]))
        h = rmsnorm(x, ple_norm[i], BF16)
        x = ple(h, bf(ple_w_gate[i]), p[i], bf(ple_w_proj[i]), x)
    return rmsnorm(x, final_norm, F32).reshape(bsz, seq, d)
```

```python
import functools

import jax
import jax.numpy as jnp
import numpy as np
from jax import lax
from jax.experimental import pallas as pl
from jax.experimental.pallas import tpu as pltpu

EPS = 1e-6
SG_CHUNK = 128
SG_HEAD_DIM = 128
GLA_HEADS = 4
GLA_GATE_RANK = 16
GLA_GATE_NORM = 16.0
TOP_K = 2

V7X_VMEM_LIMIT_BYTES = 56 * 1024 * 1024
LANES = 128
SUBLANES = 8
GLA_KERNEL_CHUNK = 128
GLA_SUB = 2 * SUBLANES
MOE_ROW_TILE = 512
LOG2E = np.float32(np.log2(np.e))

F32 = jnp.float32
BF16 = jnp.bfloat16


def _tile(dim, pref, align=LANES):
    if dim <= pref:
        return dim
    t = (pref // align) * align
    while t > align and dim % t:
        t -= align
    assert dim % t == 0, (dim, pref)
    return t


def _params(*sem):
    return pltpu.CompilerParams(dimension_semantics=sem, vmem_limit_bytes=V7X_VMEM_LIMIT_BYTES)


def _sigmoid(x):
    return 1.0 / (1.0 + jnp.exp(-x))


def _silu(x):
    return x * _sigmoid(x)


def _dot(a, b):
    return jnp.dot(a, b, preferred_element_type=F32)


def _split_bf16(x):
    hi = x.astype(BF16)
    return hi, (x - hi.astype(F32)).astype(BF16)


def _rows_scale(rstd_ref, width):
    r = rstd_ref[...]
    return r if width == LANES else jnp.concatenate([r] * (width // LANES), axis=1)


def _emit_norm_inputs(x_new, g_ref, xg_ref, rstd_ref, ss_ref, j, nj, d):
    xg_ref[...] = (x_new * g_ref[...]).astype(xg_ref.dtype)
    part = jnp.sum(x_new * x_new, axis=-1, keepdims=True)

    @pl.when(j == 0)
    def _():
        ss_ref[...] = part

    @pl.when(j > 0)
    def _():
        ss_ref[...] += part

    @pl.when(j == nj - 1)
    def _():
        rstd_ref[...] = jnp.broadcast_to(lax.rsqrt(ss_ref[...] * np.float32(1.0 / d) + EPS), rstd_ref.shape)


def _norm_inputs_kernel(x_ref, g_ref, xg_ref, rstd_ref):
    x = x_ref[...]
    xg_ref[...] = (x * g_ref[...]).astype(xg_ref.dtype)
    rstd_ref[...] = jnp.broadcast_to(lax.rsqrt(jnp.mean(x * x, axis=-1, keepdims=True) + EPS), rstd_ref.shape)


def norm_inputs(x, g):
    n, d = x.shape
    tm = _tile(n, 512, SUBLANES)
    return pl.pallas_call(
        _norm_inputs_kernel,
        grid=(n // tm,),
        in_specs=[pl.BlockSpec((tm, d), lambda i: (i, 0)), pl.BlockSpec((1, d), lambda i: (0, 0))],
        out_specs=[pl.BlockSpec((tm, d), lambda i: (i, 0)), pl.BlockSpec((tm, LANES), lambda i: (i, 0))],
        out_shape=[jax.ShapeDtypeStruct((n, d), BF16), jax.ShapeDtypeStruct((n, LANES), F32)],
        compiler_params=_params("parallel"),
        name="norm_inputs",
    )(x, g.reshape(1, d))


def _rmsnorm_kernel(x_ref, g_ref, o_ref):
    x = x_ref[...]
    o_ref[...] = x * lax.rsqrt(jnp.mean(x * x, axis=-1, keepdims=True) + EPS) * g_ref[...]


def rmsnorm(x, g):
    n, d = x.shape
    tm = _tile(n, 512, SUBLANES)
    return pl.pallas_call(
        _rmsnorm_kernel,
        grid=(n // tm,),
        in_specs=[pl.BlockSpec((tm, d), lambda i: (i, 0)), pl.BlockSpec((1, d), lambda i: (0, 0))],
        out_specs=pl.BlockSpec((tm, d), lambda i: (i, 0)),
        out_shape=jax.ShapeDtypeStruct((n, d), F32),
        compiler_params=_params("parallel"),
        name="rmsnorm",
    )(x, g.reshape(1, d))


def _mm_kernel(x_ref, rstd_ref, w_ref, o_ref, *, act):
    acc = _dot(x_ref[...], w_ref[...]) * _rows_scale(rstd_ref, o_ref.shape[1])
    if act == "gelu":
        acc = 0.5 * acc * (1.0 + lax.erf(acc * np.float32(np.sqrt(0.5))))
    o_ref[...] = acc.astype(o_ref.dtype)


def matmul(xg, rstd, w, n_out, act=None, tm=1024, tn=1024):
    n, k = xg.shape
    tm, tn = _tile(n, tm, SUBLANES), _tile(n_out, tn)
    return pl.pallas_call(
        functools.partial(_mm_kernel, act=act),
        grid=(n // tm, n_out // tn),
        in_specs=[pl.BlockSpec((tm, k), lambda i, j: (i, 0)),
                  pl.BlockSpec((tm, LANES), lambda i, j: (i, 0)),
                  pl.BlockSpec((k, tn), lambda i, j: (0, j))],
        out_specs=pl.BlockSpec((tm, tn), lambda i, j: (i, j)),
        out_shape=jax.ShapeDtypeStruct((n, n_out), BF16),
        compiler_params=_params("parallel", "parallel"),
        name="matmul_" + str(act),
    )(xg, rstd, w)


def _mm_res_kernel(x_ref, w_ref, r_ref, *rest, emit, d):
    if emit:
        g_ref, o_ref, xg_ref, rstd_ref, ss_ref = rest
    else:
        (o_ref,) = rest
    j, k = pl.program_id(1), pl.program_id(2)

    @pl.when(k == 0)
    def _():
        o_ref[...] = r_ref[...]

    o_ref[...] += _dot(x_ref[...], w_ref[...])

    if emit:
        @pl.when(k == pl.num_programs(2) - 1)
        def _():
            _emit_norm_inputs(o_ref[...], g_ref, xg_ref, rstd_ref, ss_ref, j, pl.num_programs(1), d)


def matmul_residual(x, w, res, g_next=None, tm=1024, tn=512, tk=4096):
    n, k = x.shape
    m = w.shape[1]
    tm, tn, tk = _tile(n, tm, SUBLANES), _tile(m, tn), _tile(k, tk)
    emit = g_next is not None
    in_specs = [pl.BlockSpec((tm, tk), lambda i, j, kk: (i, kk)),
                pl.BlockSpec((tk, tn), lambda i, j, kk: (kk, j)),
                pl.BlockSpec((tm, tn), lambda i, j, kk: (i, j))]
    out_specs = [pl.BlockSpec((tm, tn), lambda i, j, kk: (i, j))]
    out_shape = [jax.ShapeDtypeStruct((n, m), F32)]
    args = [x, w, res]
    scratch = []
    if emit:
        in_specs.append(pl.BlockSpec((1, tn), lambda i, j, kk: (0, j)))
        out_specs += [pl.BlockSpec((tm, tn), lambda i, j, kk: (i, j)),
                      pl.BlockSpec((tm, LANES), lambda i, j, kk: (i, 0))]
        out_shape += [jax.ShapeDtypeStruct((n, m), BF16), jax.ShapeDtypeStruct((n, LANES), F32)]
        args.append(g_next.reshape(1, m))
        scratch = [pltpu.VMEM((tm, 1), F32)]
    out = pl.pallas_call(
        functools.partial(_mm_res_kernel, emit=emit, d=m),
        grid=(n // tm, m // tn, k // tk),
        in_specs=in_specs, out_specs=out_specs, out_shape=out_shape, scratch_shapes=scratch,
        compiler_params=_params("parallel", "arbitrary", "arbitrary"),
        name="matmul_residual",
    )(*args)
    return out if emit else out[0]


def _swiglu_kernel(x_ref, rstd_ref, wg_ref, wu_ref, o_ref):
    x = x_ref[...]
    scale = _rows_scale(rstd_ref, o_ref.shape[1])
    o_ref[...] = (_silu(_dot(x, wg_ref[...]) * scale) * (_dot(x, wu_ref[...]) * scale)).astype(o_ref.dtype)


def swiglu_up(xg, rstd, w_gu, tm=1024, tn=512):
    n, k = xg.shape
    f = w_gu.shape[1] // 2
    tm, tn = _tile(n, tm, SUBLANES), _tile(f, tn)
    nj = f // tn
    return pl.pallas_call(
        _swiglu_kernel,
        grid=(n // tm, nj),
        in_specs=[pl.BlockSpec((tm, k), lambda i, j: (i, 0)),
                  pl.BlockSpec((tm, LANES), lambda i, j: (i, 0)),
                  pl.BlockSpec((k, tn), lambda i, j: (0, j)),
                  pl.BlockSpec((k, tn), lambda i, j: (0, j + nj))],
        out_specs=pl.BlockSpec((tm, tn), lambda i, j: (i, j)),
        out_shape=jax.ShapeDtypeStruct((n, f), BF16),
        compiler_params=_params("parallel", "parallel"),
        name="swiglu_up",
    )(xg, rstd, w_gu, w_gu)


def _ple_kernel(xg_ref, rstd_ref, wg_ref, p_ref, wp_ref, x_ref, *rest, emit, d):
    if emit:
        g_ref, o_ref, xg_out_ref, rstd_out_ref, ss_ref = rest
    else:
        (o_ref,) = rest
    gate = _sigmoid(_dot(xg_ref[...], wg_ref[...]) * _rows_scale(rstd_ref, o_ref.shape[1]))
    x_new = x_ref[...] + gate * _dot(p_ref[...], wp_ref[...])
    o_ref[...] = x_new
    if emit:
        _emit_norm_inputs(x_new, g_ref, xg_out_ref, rstd_out_ref, ss_ref, pl.program_id(1), pl.num_programs(1), d)


def ple(xg, rstd, w_gate, p, w_proj, x, g_next=None, tm=1024, tn=512):
    n, d = xg.shape
    pd = p.shape[1]
    m = w_gate.shape[1]
    tm, tn = _tile(n, tm, SUBLANES), _tile(m, tn)
    emit = g_next is not None
    in_specs = [pl.BlockSpec((tm, d), lambda i, j: (i, 0)),
                pl.BlockSpec((tm, LANES), lambda i, j: (i, 0)),
                pl.BlockSpec((d, tn), lambda i, j: (0, j)),
                pl.BlockSpec((tm, pd), lambda i, j: (i, 0)),
                pl.BlockSpec((pd, tn), lambda i, j: (0, j)),
                pl.BlockSpec((tm, tn), lambda i, j: (i, j))]
    out_specs = [pl.BlockSpec((tm, tn), lambda i, j: (i, j))]
    out_shape = [jax.ShapeDtypeStruct((n, m), F32)]
    args = [xg, rstd, w_gate, p, w_proj, x]
    scratch = []
    if emit:
        in_specs.append(pl.BlockSpec((1, tn), lambda i, j: (0, j)))
        out_specs += [pl.BlockSpec((tm, tn), lambda i, j: (i, j)), pl.BlockSpec((tm, LANES), lambda i, j: (i, 0))]
        out_shape += [jax.ShapeDtypeStruct((n, m), BF16), jax.ShapeDtypeStruct((n, LANES), F32)]
        args.append(g_next.reshape(1, m))
        scratch = [pltpu.VMEM((tm, 1), F32)]
    out = pl.pallas_call(
        functools.partial(_ple_kernel, emit=emit, d=m),
        grid=(n // tm, m // tn),
        in_specs=in_specs, out_specs=out_specs, out_shape=out_shape, scratch_shapes=scratch,
        compiler_params=_params("parallel", "arbitrary"),
        name="ple",
    )(*args)
    return out if emit else out[0]


def _sg_kernel(u_ref, v_ref, gv_ref, ws_ref, bias_ref, o_ref, *, heads):
    tb = u_ref.shape[0]
    t, hd = SG_CHUNK, SG_HEAD_DIM
    v = v_ref[...].astype(F32)
    vn = (v * lax.rsqrt(jnp.mean(v * v, axis=-1, keepdims=True) + EPS) * gv_ref[...]).astype(BF16)
    causal = lax.broadcasted_iota(jnp.int32, (t, t), 0) >= lax.broadcasted_iota(jnp.int32, (t, t), 1)
    for h in range(heads):
        cols = slice(h * hd, (h + 1) * hd)
        wc = jnp.where(causal, ws_ref[h], 0.0).astype(BF16)
        bias = bias_ref[:, cols]
        for c in range(tb // t):
            rows = slice(c * t, (c + 1) * t)
            mixed = _dot(wc, vn[rows, cols]) + bias
            o_ref[rows, cols] = (u_ref[rows, cols].astype(F32) * mixed).astype(o_ref.dtype)


def spatial_gate(z, g_v, w_s, b_s, tb=512):
    n, w2 = z.shape
    w = w2 // 2
    heads = w // SG_HEAD_DIM
    tb = _tile(n, tb, SG_CHUNK)
    bias = jnp.repeat(b_s.T, SG_HEAD_DIM, axis=1)
    return pl.pallas_call(
        functools.partial(_sg_kernel, heads=heads),
        grid=(n // tb,),
        in_specs=[pl.BlockSpec((tb, w), lambda i: (i, 0)),
                  pl.BlockSpec((tb, w), lambda i: (i, 1)),
                  pl.BlockSpec((1, w), lambda i: (0, 0)),
                  pl.BlockSpec((heads, SG_CHUNK, SG_CHUNK), lambda i: (0, 0, 0)),
                  pl.BlockSpec((SG_CHUNK, w), lambda i: (0, 0))],
        out_specs=pl.BlockSpec((tb, w), lambda i: (i, 0)),
        out_shape=jax.ShapeDtypeStruct((n, w), BF16),
        compiler_params=_params("parallel"),
        name="spatial_gate",
    )(z, z, g_v.reshape(1, w), w_s, bias)


def _gate_proj_kernel(x_ref, rstd_ref, w_ref, o_ref, *, rank):
    a = _dot(x_ref[...], w_ref[...]) * rstd_ref[...]
    o_ref[...] = jnp.where(lax.broadcasted_iota(jnp.int32, a.shape, 1) < rank, a, 0.0)


def gla_gate_proj(xg, rstd, w_in, col0, rank, tm=1024):
    n, d = xg.shape
    assert col0 % LANES == 0
    tm = _tile(n, tm, SUBLANES)
    return pl.pallas_call(
        functools.partial(_gate_proj_kernel, rank=rank),
        grid=(n // tm,),
        in_specs=[pl.BlockSpec((tm, d), lambda i: (i, 0)),
                  pl.BlockSpec((tm, LANES), lambda i: (i, 0)),
                  pl.BlockSpec((d, LANES), lambda i: (0, col0 // LANES))],
        out_specs=pl.BlockSpec((tm, LANES), lambda i: (i, 0)),
        out_shape=jax.ShapeDtypeStruct((n, LANES), F32),
        compiler_params=_params("parallel"),
        name="gla_gate_proj",
    )(xg, rstd, w_in)


def _chunk_log2_decay(a, wa_ref, ba_ref):
    c = a.shape[0]
    z2 = (_dot(a.astype(BF16), wa_ref[...]) + ba_ref[...]) * LOG2E
    g2 = (jnp.minimum(z2, 0.0) - jnp.log2(1.0 + jnp.exp2(-jnp.abs(z2)))) * np.float32(1.0 / GLA_GATE_NORM)
    tri = (lax.broadcasted_iota(jnp.int32, (c, c), 0) >= lax.broadcasted_iota(jnp.int32, (c, c), 1)).astype(BF16)
    g_hi, g_lo = _split_bf16(g2)
    return _dot(tri, g_hi) + _dot(tri, g_lo)


def _gla_kernel(q_ref, k_ref, v_ref, r_ref, a_ref, wa_ref, ba_ref, go_ref, o_ref, s_ref, sc_ref, *, scale):
    c, dk = q_ref.shape
    sub, half = GLA_SUB, SUBLANES

    @pl.when(pl.program_id(2) == 0)
    def _():
        s_ref[...] = jnp.zeros_like(s_ref)

    b = _chunk_log2_decay(a_ref[...], wa_ref, ba_ref)
    q = q_ref[...].astype(F32) * scale
    k = k_ref[...].astype(F32)
    v = v_ref[...]
    o = _dot((q * jnp.exp2(b)).astype(BF16), s_ref[...].astype(BF16))

    lane = lax.broadcasted_iota(jnp.int32, (half, c), 1)
    t_loc = lax.broadcasted_iota(jnp.int32, (half, 1), 0)
    for i in range(c // sub):
        i0 = i * sub
        qi, bi, ki = q[i0:i0 + sub], b[i0:i0 + sub], k[i0:i0 + sub]
        a_top = jnp.zeros((half, c), F32)
        a_bot = jnp.zeros((half, c), F32)
        if i > 0:
            ref = b[i0 - 1:i0]
            qt = (qi * jnp.exp2(bi - ref)).astype(BF16)
            kt = (k * jnp.exp2(jnp.minimum(ref - b, 0.0))).astype(BF16)
            a_off = lax.dot_general(qt, kt, (((1,), (1,)), ((), ())), preferred_element_type=F32)
            a_top = jnp.where(lane < i0, a_off[:half], 0.0)
            a_bot = jnp.where(lane < i0, a_off[half:], 0.0)
        q_top, b_top, q_bot, b_bot = qi[:half], bi[:half], qi[half:], bi[half:]
        for s in range(sub):
            ks, bs = ki[s:s + 1], bi[s:s + 1]
            col = jnp.sum(q_bot * ks * jnp.exp2(jnp.minimum(b_bot - bs, 0.0)), axis=-1, keepdims=True)
            if s >= half:
                col = jnp.where(t_loc >= s - half, col, 0.0)
            a_bot = a_bot + jnp.where(lane == i0 + s, col, 0.0)
            if s < half:
                col = jnp.sum(q_top * ks * jnp.exp2(jnp.minimum(b_top - bs, 0.0)), axis=-1, keepdims=True)
                a_top = a_top + jnp.where(lane == i0 + s, jnp.where(t_loc >= s, col, 0.0), 0.0)
        sc_ref[i0:i0 + half, :] = a_top
        sc_ref[i0 + half:i0 + sub, :] = a_bot
    o = o + _dot(sc_ref[...].astype(BF16), v)

    b_last = b[c - 1:c]
    k_dec = (k * jnp.exp2(b_last - b)).astype(BF16)
    upd = lax.dot_general(k_dec, v, (((0,), (0,)), ((), ())), preferred_element_type=F32)
    decay_col = jnp.transpose(jnp.broadcast_to(jnp.exp2(b_last), (LANES, dk)))[:, 0:1]
    s_ref[...] = decay_col * s_ref[...] + upd

    r = r_ref[...].astype(F32)
    on = o * lax.rsqrt(jnp.mean(o * o, axis=-1, keepdims=True) + EPS) * go_ref[...]
    o_ref[...] = (on * _silu(r)).astype(o_ref.dtype)


def gla(qkvr, a_low, w_alpha, b_alpha, g_o, bsz, seq):
    n = qkvr.shape[0]
    rank, kd = w_alpha.shape
    vd = (qkvr.shape[1] - 2 * kd) // 2
    dk, dv = kd // GLA_HEADS, vd // GLA_HEADS
    c = _tile(seq, GLA_KERNEL_CHUNK, GLA_SUB)
    nc = seq // c
    assert kd % dk == 0 and (2 * kd) % dv == 0 and (2 * kd + vd) % dv == 0
    k_blk, v_blk, r_blk = kd // dk, (2 * kd) // dv, (2 * kd + vd) // dv
    w_alpha = jnp.pad(w_alpha, ((0, LANES - rank), (0, 0))).astype(BF16)

    def row(b, h, j):
        return b * nc + j

    return pl.pallas_call(
        functools.partial(_gla_kernel, scale=np.float32(dk ** -0.5)),
        grid=(bsz, GLA_HEADS, nc),
        in_specs=[pl.BlockSpec((c, dk), lambda b, h, j: (row(b, h, j), h)),
                  pl.BlockSpec((c, dk), lambda b, h, j: (row(b, h, j), k_blk + h)),
                  pl.BlockSpec((c, dv), lambda b, h, j: (row(b, h, j), v_blk + h)),
                  pl.BlockSpec((c, dv), lambda b, h, j: (row(b, h, j), r_blk + h)),
                  pl.BlockSpec((c, LANES), lambda b, h, j: (row(b, h, j), 0)),
                  pl.BlockSpec((LANES, dk), lambda b, h, j: (0, h)),
                  pl.BlockSpec((1, dk), lambda b, h, j: (0, h)),
                  pl.BlockSpec((1, dv), lambda b, h, j: (0, h))],
        out_specs=pl.BlockSpec((c, dv), lambda b, h, j: (row(b, h, j), h)),
        out_shape=jax.ShapeDtypeStruct((n, vd), BF16),
        scratch_shapes=[pltpu.VMEM((dk, dv), F32), pltpu.VMEM((c, c), F32)],
        compiler_params=_params("parallel", "parallel", "arbitrary"),
        name="gla",
    )(qkvr, qkvr, qkvr, qkvr, a_low, w_alpha, b_alpha.reshape(1, kd), g_o.reshape(1, vd))


def _router_kernel(x_ref, g_ref, wr_ref, h_ref, gates_ref, sel_ref, *, n_experts):
    x = x_ref[...]
    h = x * lax.rsqrt(jnp.mean(x * x, axis=-1, keepdims=True) + EPS) * g_ref[...]
    h_ref[...] = h
    logits = jnp.dot(h, wr_ref[...], precision=lax.Precision.HIGHEST, preferred_element_type=F32)
    lane = lax.broadcasted_iota(jnp.int32, logits.shape, 1)
    neg = -jnp.inf
    logits = jnp.where(lane < n_experts, logits, neg)
    m1 = jnp.max(logits, axis=-1, keepdims=True)
    i1 = jnp.min(jnp.where(logits == m1, lane, LANES), axis=-1, keepdims=True)
    is1 = lane == i1
    rest = jnp.where(is1, neg, logits)
    m2 = jnp.max(rest, axis=-1, keepdims=True)
    i2 = jnp.min(jnp.where(rest == m2, lane, LANES), axis=-1, keepdims=True)
    is2 = lane == i2
    e2 = jnp.exp(m2 - m1)
    den = 1.0 + e2
    gates_ref[...] = jnp.where(is1, 1.0 / den, 0.0) + jnp.where(is2, e2 / den, 0.0)
    sel_ref[...] = jnp.where(is1 | is2, 1, 0).astype(jnp.int32)


def moe_router(x, g, w_router, tm=256):
    n, d = x.shape
    e = w_router.shape[1]
    tm = _tile(n, tm, SUBLANES)
    wr = jnp.pad(w_router, ((0, 0), (0, LANES - e)))
    return pl.pallas_call(
        functools.partial(_router_kernel, n_experts=e),
        grid=(n // tm,),
        in_specs=[pl.BlockSpec((tm, d), lambda i: (i, 0)),
                  pl.BlockSpec((1, d), lambda i: (0, 0)),
                  pl.BlockSpec((d, LANES), lambda i: (0, 0))],
        out_specs=[pl.BlockSpec((tm, d), lambda i: (i, 0)),
                   pl.BlockSpec((tm, LANES), lambda i: (i, 0)),
                   pl.BlockSpec((tm, LANES), lambda i: (i, 0))],
        out_shape=[jax.ShapeDtypeStruct((n, d), F32),
                   jax.ShapeDtypeStruct((n, LANES), F32),
                   jax.ShapeDtypeStruct((n, LANES), jnp.int32)],
        compiler_params=_params("parallel"),
        name="moe_router",
    )(x, g.reshape(1, d), wr)


def _row_copy(src_ref, src_row, dst_ref, dst_row, sem):
    return pltpu.make_async_copy(src_ref.at[pl.ds(src_row, 1)], dst_ref.at[pl.ds(dst_row, 1)], sem)


def _dispatch_kernel(dest_ref, h_ref, xs_in_ref, xs_ref, sem):
    del xs_in_ref
    tm = h_ref.shape[0]

    def start(r, carry):
        for kk in range(TOP_K):
            _row_copy(h_ref, r, xs_ref, dest_ref[0, 0, TOP_K * r + kk], sem).start()
        return carry

    lax.fori_loop(0, tm, start, 0)

    def wait(r, carry):
        for kk in range(TOP_K):
            _row_copy(h_ref, 0, xs_ref, 0, sem).wait()
        return carry

    lax.fori_loop(0, tm, wait, 0)


def moe_dispatch(h, dest, n_rows, tm=256):
    n, d = h.shape
    tm = _tile(n, tm, SUBLANES)
    dest3 = dest.reshape(n // tm, 1, TOP_K * tm)
    zeros = jnp.zeros((n_rows, d), h.dtype)
    return pl.pallas_call(
        _dispatch_kernel,
        grid=(n // tm,),
        in_specs=[pl.BlockSpec((1, 1, TOP_K * tm), lambda i: (i, 0, 0), memory_space=pltpu.SMEM),
                  pl.BlockSpec((tm, d), lambda i: (i, 0)),
                  pl.BlockSpec(memory_space=pl.ANY)],
        out_specs=pl.BlockSpec(memory_space=pl.ANY),
        out_shape=jax.ShapeDtypeStruct((n_rows, d), h.dtype),
        scratch_shapes=[pltpu.SemaphoreType.DMA(())],
        input_output_aliases={2: 0},
        compiler_params=_params("arbitrary"),
        name="moe_dispatch",
    )(dest3, h, zeros)


def _moe_up_kernel(te_ref, nt_ref, x_ref, wg_ref, wu_ref, o_ref, xb_ref):
    i, j = pl.program_id(0), pl.program_id(1)

    @pl.when(i < nt_ref[0])
    def _():
        @pl.when(j == 0)
        def _():
            xb_ref[...] = x_ref[...].astype(BF16)

        x = xb_ref[...]
        o_ref[...] = (_silu(_dot(x, wg_ref[0])) * _dot(x, wu_ref[0])).astype(o_ref.dtype)

    @pl.when(i >= nt_ref[0])
    def _():
        o_ref[...] = jnp.zeros_like(o_ref)


def moe_up(xs, w_gu, tile_expert, n_tiles, tn=512):
    rows, d = xs.shape
    f = w_gu.shape[2] // 2
    tm, tn = MOE_ROW_TILE, _tile(f, tn)
    nj = f // tn
    grid_spec = pltpu.PrefetchScalarGridSpec(
        num_scalar_prefetch=2,
        grid=(rows // tm, nj),
        in_specs=[pl.BlockSpec((tm, d), lambda i, j, te, nt: (i, 0)),
                  pl.BlockSpec((1, d, tn), lambda i, j, te, nt: (te[i], 0, j)),
                  pl.BlockSpec((1, d, tn), lambda i, j, te, nt: (te[i], 0, j + nj))],
        out_specs=pl.BlockSpec((tm, tn), lambda i, j, te, nt: (i, j)),
        scratch_shapes=[pltpu.VMEM((tm, d), BF16)],
    )
    return pl.pallas_call(
        _moe_up_kernel,
        grid_spec=grid_spec,
        out_shape=jax.ShapeDtypeStruct((rows, f), BF16),
        compiler_params=_params("parallel", "arbitrary"),
        name="moe_up",
    )(tile_expert, n_tiles, xs, w_gu, w_gu)


def _moe_down_kernel(te_ref, nt_ref, a_ref, w_ref, o_ref):
    i = pl.program_id(0)

    @pl.when(i < nt_ref[0])
    def _():
        o_ref[...] = _dot(a_ref[...], w_ref[0])

    @pl.when(i >= nt_ref[0])
    def _():
        o_ref[...] = jnp.zeros_like(o_ref)


def moe_down(a, w_down, tile_expert, n_tiles, tn=1024):
    rows, f = a.shape
    d = w_down.shape[2]
    tm, tn = MOE_ROW_TILE, _tile(d, tn)
    grid_spec = pltpu.PrefetchScalarGridSpec(
        num_scalar_prefetch=2,
        grid=(rows // tm, d // tn),
        in_specs=[pl.BlockSpec((tm, f), lambda i, j, te, nt: (i, 0)),
                  pl.BlockSpec((1, f, tn), lambda i, j, te, nt: (te[i], 0, j))],
        out_specs=pl.BlockSpec((tm, tn), lambda i, j, te, nt: (i, j)),
    )
    return pl.pallas_call(
        _moe_down_kernel,
        grid_spec=grid_spec,
        out_shape=jax.ShapeDtypeStruct((rows, d), F32),
        compiler_params=_params("parallel", "parallel"),
        name="moe_down",
    )(tile_expert, n_tiles, a, w_down)


def _combine_kernel(dest_ref, x_ref, w_ref, y_ref, g_ref, o_ref, xg_ref, rstd_ref, buf_ref, sem):
    tm = x_ref.shape[0]

    def start(r, carry):
        for kk in range(TOP_K):
            _row_copy(y_ref, dest_ref[0, 0, TOP_K * r + kk], buf_ref.at[kk], r, sem).start()
        return carry

    lax.fori_loop(0, tm, start, 0)

    def wait(r, carry):
        for kk in range(TOP_K):
            _row_copy(y_ref, 0, buf_ref.at[kk], 0, sem).wait()
        return carry

    lax.fori_loop(0, tm, wait, 0)
    w = w_ref[...]
    x_new = x_ref[...] + (w[:, 0:1] * buf_ref[0] + w[:, 1:2] * buf_ref[1])
    o_ref[...] = x_new
    xg_ref[...] = (x_new * g_ref[...]).astype(xg_ref.dtype)
    rstd_ref[...] = jnp.broadcast_to(lax.rsqrt(jnp.mean(x_new * x_new, axis=-1, keepdims=True) + EPS), rstd_ref.shape)


def moe_combine(x, y, dest, w, g_next, tm=256):
    n, d = x.shape
    tm = _tile(n, tm, SUBLANES)
    dest3 = dest.reshape(n // tm, 1, TOP_K * tm)
    return pl.pallas_call(
        _combine_kernel,
        grid=(n // tm,),
        in_specs=[pl.BlockSpec((1, 1, TOP_K * tm), lambda i: (i, 0, 0), memory_space=pltpu.SMEM),
                  pl.BlockSpec((tm, d), lambda i: (i, 0)),
                  pl.BlockSpec((tm, LANES), lambda i: (i, 0)),
                  pl.BlockSpec(memory_space=pl.ANY),
                  pl.BlockSpec((1, d), lambda i: (0, 0))],
        out_specs=[pl.BlockSpec((tm, d), lambda i: (i, 0)),
                   pl.BlockSpec((tm, d), lambda i: (i, 0)),
                   pl.BlockSpec((tm, LANES), lambda i: (i, 0))],
        out_shape=[jax.ShapeDtypeStruct((n, d), F32),
                   jax.ShapeDtypeStruct((n, d), BF16),
                   jax.ShapeDtypeStruct((n, LANES), F32)],
        scratch_shapes=[pltpu.VMEM((TOP_K, tm, d), F32), pltpu.SemaphoreType.DMA(())],
        compiler_params=_params("arbitrary"),
        name="moe_combine",
    )(dest3, x, w, y, g_next.reshape(1, d))


def moe_layer(x, g, w_router, w_gu, w_down, g_next):
    n, d = x.shape
    n_exp = w_router.shape[1]
    tm = MOE_ROW_TILE
    h, gates, sel = moe_router(x, g, w_router)
    gates, sel = gates[:, :n_exp], sel[:, :n_exp]

    csum = jnp.cumsum(sel, axis=0)
    counts = csum[-1]
    group = ((counts + tm - 1) // tm) * tm
    ends = jnp.cumsum(group)
    slot = (ends - group)[None, :] + csum - 1
    big = jnp.iinfo(jnp.int32).max
    d_lo = jnp.min(jnp.where(sel > 0, slot, big), axis=1)
    d_hi = jnp.max(jnp.where(sel > 0, slot, -1), axis=1)
    w_lo = jnp.sum(jnp.where((sel > 0) & (slot == d_lo[:, None]), gates, 0.0), axis=1)
    w_hi = jnp.sum(jnp.where((sel > 0) & (slot == d_hi[:, None]), gates, 0.0), axis=1)
    dest = jnp.stack([d_lo, d_hi], axis=1).astype(jnp.int32)
    w = jnp.pad(jnp.stack([w_lo, w_hi], axis=1), ((0, 0), (0, LANES - TOP_K)))

    n_rows = TOP_K * n + n_exp * tm
    n_row_tiles = n_rows // tm
    tile_start = jnp.arange(n_row_tiles, dtype=jnp.int32) * tm
    tile_expert = jnp.minimum(jnp.sum((ends[None, :] <= tile_start[:, None]).astype(jnp.int32), axis=1), n_exp - 1)
    n_tiles = (ends[-1:] // tm).astype(jnp.int32)

    xs = moe_dispatch(h, dest, n_rows)
    a = moe_up(xs, w_gu, tile_expert, n_tiles)
    y = moe_down(a, w_down, tile_expert, n_tiles)
    return moe_combine(x, y, dest, w, g_next)


def kernel(x, p, mix_norm, ffn_norm, ple_norm, ple_w_gate, ple_w_proj, sg_w_in, sg_g_v, sg_w_s, sg_b_s, sg_w_out,
           ffn_w_gu, ffn_w_down, gla_w_in, gla_w_alpha, gla_b_alpha, gla_g_o, gla_w_out, moe_w_router, moe_w_gu,
           moe_w_down, final_norm):
    bsz, seq, d = x.shape
    n = bsz * seq
    depth = p.shape[0]
    x = x.reshape(n, d)
    p = p.reshape(depth, n, -1).astype(BF16)
    bf = lambda w: w.astype(BF16)

    xg, rstd = norm_inputs(x, mix_norm[0])
    for i in range(depth):
        j = i // 2
        last = i == depth - 1
        if i % 2 == 0:
            z = matmul(xg, rstd, bf(sg_w_in[j]), sg_w_in.shape[2], act="gelu")
            gated = spatial_gate(z, sg_g_v[j], sg_w_s[j], sg_b_s[j])
            x, xg, rstd = matmul_residual(gated, bf(sg_w_out[j]), x, ffn_norm[i])
            a = swiglu_up(xg, rstd, bf(ffn_w_gu[j]))
            x, xg, rstd = matmul_residual(a, bf(ffn_w_down[j]), x, ple_norm[i], tn=1024, tk=2048)
        else:
            w_in = bf(gla_w_in[j])
            n_main = w_in.shape[1] - GLA_GATE_RANK
            qkvr = matmul(xg, rstd, w_in, n_main)
            a_low = gla_gate_proj(xg, rstd, w_in, n_main, GLA_GATE_RANK)
            og = gla(qkvr, a_low, gla_w_alpha[j], gla_b_alpha[j], gla_g_o[j], bsz, seq)
            x = matmul_residual(og, bf(gla_w_out[j]), x)
            x, xg, rstd = moe_layer(x, ffn_norm[i], moe_w_router[j], bf(moe_w_gu[j]), bf(moe_w_down[j]), ple_norm[i])
        if last:
            x = ple(xg, rstd, bf(ple_w_gate[i]), p[i], bf(ple_w_proj[i]), x)
        else:
            x, xg, rstd = ple(xg, rstd, bf(ple_w_gate[i]), p[i], bf(ple_w_proj[i]), x, mix_norm[i + 1])
    return rmsnorm(x, final_norm).reshape(bsz, seq, d)
```

```python
import functools

import jax
import jax.numpy as jnp
import numpy as np
from jax import lax
from jax.experimental import pallas as pl
from jax.experimental.pallas import tpu as pltpu

EPS = 1e-6
SG_CHUNK = 128
SG_HEAD_DIM = 128
GLA_HEADS = 4
GLA_GATE_RANK = 16
GLA_GATE_NORM = 16.0
TOP_K = 2

V7X_VMEM_LIMIT_BYTES = 56 * 1024 * 1024
LANES = 128
SUBLANES = 8
ROW_BLOCK = 128
GLA_KERNEL_CHUNK = 128
GLA_SUB = 2 * SUBLANES
MOE_ROW_TILE = 512
LOG2E = np.float32(np.log2(np.e))

F32 = jnp.float32
BF16 = jnp.bfloat16
U32 = jnp.uint32


def _tile(dim, pref, align=LANES):
    if dim <= pref:
        return dim
    t = (pref // align) * align
    while t > align and dim % t:
        t -= align
    assert dim % t == 0, (dim, pref)
    return t


def _row_blocks(tm):
    rb = min(ROW_BLOCK, tm)
    assert tm % rb == 0
    return [pl.ds(r0, rb) for r0 in range(0, tm, rb)]


def _params(*sem):
    return pltpu.CompilerParams(dimension_semantics=sem, vmem_limit_bytes=V7X_VMEM_LIMIT_BYTES)


def _sigmoid(x):
    return 1.0 / (1.0 + jnp.exp(-x))


def _silu(x):
    return x * _sigmoid(x)


def _dot(a, b):
    return jnp.dot(a, b, preferred_element_type=F32)


def _split_bf16(x):
    hi = x.astype(BF16)
    return hi, (x - hi.astype(F32)).astype(BF16)


def _lane_tiles(r, width):
    return r if width == LANES else jnp.concatenate([r] * (width // LANES), axis=1)


def _emit_norm_inputs(x_new, rows, g_ref, xg_ref, ss_ref, j):
    xg_ref[rows, :] = (x_new * g_ref[...]).astype(xg_ref.dtype)
    part = jnp.sum(x_new * x_new, axis=-1, keepdims=True)
    ss_ref[rows, :] = jnp.where(j == 0, part, ss_ref[rows, :] + part)


def _emit_rstd(ss_ref, rstd_ref, j, nj, d):
    @pl.when(j == nj - 1)
    def _():
        rstd_ref[...] = jnp.broadcast_to(lax.rsqrt(ss_ref[...] * np.float32(1.0 / d) + EPS), rstd_ref.shape)


def _norm_inputs_kernel(x_ref, g_ref, xg_ref, rstd_ref):
    x = x_ref[...]
    xg_ref[...] = (x * g_ref[...]).astype(xg_ref.dtype)
    rstd_ref[...] = jnp.broadcast_to(lax.rsqrt(jnp.mean(x * x, axis=-1, keepdims=True) + EPS), rstd_ref.shape)


def norm_inputs(x, g):
    n, d = x.shape
    tm = _tile(n, 512, SUBLANES)
    return pl.pallas_call(
        _norm_inputs_kernel,
        grid=(n // tm,),
        in_specs=[pl.BlockSpec((tm, d), lambda i: (i, 0)), pl.BlockSpec((1, d), lambda i: (0, 0))],
        out_specs=[pl.BlockSpec((tm, d), lambda i: (i, 0)), pl.BlockSpec((tm, LANES), lambda i: (i, 0))],
        out_shape=[jax.ShapeDtypeStruct((n, d), BF16), jax.ShapeDtypeStruct((n, LANES), F32)],
        compiler_params=_params("parallel"),
        name="norm_inputs",
    )(x, g.reshape(1, d))


def _rmsnorm_kernel(x_ref, g_ref, o_ref):
    x = x_ref[...]
    o_ref[...] = x * lax.rsqrt(jnp.mean(x * x, axis=-1, keepdims=True) + EPS) * g_ref[...]


def rmsnorm(x, g):
    n, d = x.shape
    tm = _tile(n, 512, SUBLANES)
    return pl.pallas_call(
        _rmsnorm_kernel,
        grid=(n // tm,),
        in_specs=[pl.BlockSpec((tm, d), lambda i: (i, 0)), pl.BlockSpec((1, d), lambda i: (0, 0))],
        out_specs=pl.BlockSpec((tm, d), lambda i: (i, 0)),
        out_shape=jax.ShapeDtypeStruct((n, d), F32),
        compiler_params=_params("parallel"),
        name="rmsnorm",
    )(x, g.reshape(1, d))


def _mm_kernel(x_ref, rstd_ref, w_ref, o_ref, *, act):
    tm, tn = o_ref.shape
    for rows in _row_blocks(tm):
        acc = _dot(x_ref[rows, :], w_ref[...]) * _lane_tiles(rstd_ref[rows, :], tn)
        if act == "gelu":
            acc = 0.5 * acc * (1.0 + lax.erf(acc * np.float32(np.sqrt(0.5))))
        o_ref[rows, :] = acc.astype(o_ref.dtype)


def matmul(xg, rstd, w, n_out, act=None, tm=1024, tn=1024):
    n, k = xg.shape
    tm, tn = _tile(n, tm, SUBLANES), _tile(n_out, tn)
    return pl.pallas_call(
        functools.partial(_mm_kernel, act=act),
        grid=(n // tm, n_out // tn),
        in_specs=[pl.BlockSpec((tm, k), lambda i, j: (i, 0)),
                  pl.BlockSpec((tm, LANES), lambda i, j: (i, 0)),
                  pl.BlockSpec((k, tn), lambda i, j: (0, j))],
        out_specs=pl.BlockSpec((tm, tn), lambda i, j: (i, j)),
        out_shape=jax.ShapeDtypeStruct((n, n_out), BF16),
        compiler_params=_params("parallel", "parallel"),
        name="matmul_" + str(act),
    )(xg, rstd, w)


def _mm_res_kernel(x_ref, w_ref, r_ref, *rest, emit, nk, d):
    if emit:
        g_ref, o_ref, xg_ref, rstd_ref, ss_ref = rest
    else:
        (o_ref,) = rest
    j, k = pl.program_id(1), pl.program_id(2)

    def step(first, last):
        for rows in _row_blocks(o_ref.shape[0]):
            base = r_ref[rows, :] if first else o_ref[rows, :]
            x_new = base + _dot(x_ref[rows, :], w_ref[...])
            o_ref[rows, :] = x_new
            if last and emit:
                _emit_norm_inputs(x_new, rows, g_ref, xg_ref, ss_ref, j)
        if last and emit:
            _emit_rstd(ss_ref, rstd_ref, j, pl.num_programs(1), d)

    if nk == 1:
        step(True, True)
    else:
        pl.when(k == 0)(lambda: step(True, False))
        pl.when((k > 0) & (k < nk - 1))(lambda: step(False, False))
        pl.when(k == nk - 1)(lambda: step(False, True))


def matmul_residual(x, w, res, g_next=None, tm=1024, tn=512, tk=4096):
    n, k = x.shape
    m = w.shape[1]
    tm, tn, tk = _tile(n, tm, SUBLANES), _tile(m, tn), _tile(k, tk)
    nk = k // tk
    emit = g_next is not None
    in_specs = [pl.BlockSpec((tm, tk), lambda i, j, kk: (i, kk)),
                pl.BlockSpec((tk, tn), lambda i, j, kk: (kk, j)),
                pl.BlockSpec((tm, tn), lambda i, j, kk: (i, j))]
    out_specs = [pl.BlockSpec((tm, tn), lambda i, j, kk: (i, j))]
    out_shape = [jax.ShapeDtypeStruct((n, m), F32)]
    args = [x, w, res]
    scratch = []
    if emit:
        in_specs.append(pl.BlockSpec((1, tn), lambda i, j, kk: (0, j)))
        out_specs += [pl.BlockSpec((tm, tn), lambda i, j, kk: (i, j)),
                      pl.BlockSpec((tm, LANES), lambda i, j, kk: (i, 0))]
        out_shape += [jax.ShapeDtypeStruct((n, m), BF16), jax.ShapeDtypeStruct((n, LANES), F32)]
        args.append(g_next.reshape(1, m))
        scratch = [pltpu.VMEM((tm, 1), F32)]
    out = pl.pallas_call(
        functools.partial(_mm_res_kernel, emit=emit, nk=nk, d=m),
        grid=(n // tm, m // tn, nk),
        in_specs=in_specs, out_specs=out_specs, out_shape=out_shape, scratch_shapes=scratch,
        compiler_params=_params("parallel", "arbitrary", "arbitrary"),
        name="matmul_residual",
    )(*args)
    return out if emit else out[0]


def _swiglu_kernel(x_ref, rstd_ref, wg_ref, wu_ref, o_ref):
    tm, tn = o_ref.shape
    for rows in _row_blocks(tm):
        x = x_ref[rows, :]
        scale = _lane_tiles(rstd_ref[rows, :], tn)
        o_ref[rows, :] = (_silu(_dot(x, wg_ref[...]) * scale) * (_dot(x, wu_ref[...]) * scale)).astype(o_ref.dtype)


def swiglu_up(xg, rstd, w_gu, tm=1024, tn=512):
    n, k = xg.shape
    f = w_gu.shape[1] // 2
    tm, tn = _tile(n, tm, SUBLANES), _tile(f, tn)
    nj = f // tn
    return pl.pallas_call(
        _swiglu_kernel,
        grid=(n // tm, nj),
        in_specs=[pl.BlockSpec((tm, k), lambda i, j: (i, 0)),
                  pl.BlockSpec((tm, LANES), lambda i, j: (i, 0)),
                  pl.BlockSpec((k, tn), lambda i, j: (0, j)),
                  pl.BlockSpec((k, tn), lambda i, j: (0, j + nj))],
        out_specs=pl.BlockSpec((tm, tn), lambda i, j: (i, j)),
        out_shape=jax.ShapeDtypeStruct((n, f), BF16),
        compiler_params=_params("parallel", "parallel"),
        name="swiglu_up",
    )(xg, rstd, w_gu, w_gu)


def _ple_kernel(xg_ref, rstd_ref, wg_ref, p_ref, wp_ref, x_ref, *rest, emit, d):
    if emit:
        g_ref, o_ref, xg_out_ref, rstd_out_ref, ss_ref = rest
    else:
        (o_ref,) = rest
    tm, tn = o_ref.shape
    j = pl.program_id(1)
    for rows in _row_blocks(tm):
        gate = _sigmoid(_dot(xg_ref[rows, :], wg_ref[0]) * _lane_tiles(rstd_ref[rows, :], tn))
        x_new = x_ref[rows, :] + gate * _dot(p_ref[rows, :], wp_ref[0])
        o_ref[rows, :] = x_new
        if emit:
            _emit_norm_inputs(x_new, rows, g_ref, xg_out_ref, ss_ref, j)
    if emit:
        _emit_rstd(ss_ref, rstd_out_ref, j, pl.num_programs(1), d)


def ple(xg, rstd, w_gate, w_proj, layer, p, x, g_next=None, tm=1024, tn=512):
    n, d = xg.shape
    pd = p.shape[1]
    m = w_gate.shape[2]
    tm, tn = _tile(n, tm, SUBLANES), _tile(m, tn)
    emit = g_next is not None
    in_specs = [pl.BlockSpec((tm, d), lambda i, j: (i, 0)),
                pl.BlockSpec((tm, LANES), lambda i, j: (i, 0)),
                pl.BlockSpec((1, d, tn), lambda i, j: (layer, 0, j)),
                pl.BlockSpec((tm, pd), lambda i, j: (i, 0)),
                pl.BlockSpec((1, pd, tn), lambda i, j: (layer, 0, j)),
                pl.BlockSpec((tm, tn), lambda i, j: (i, j))]
    out_specs = [pl.BlockSpec((tm, tn), lambda i, j: (i, j))]
    out_shape = [jax.ShapeDtypeStruct((n, m), F32)]
    args = [xg, rstd, w_gate, p, w_proj, x]
    scratch = []
    if emit:
        in_specs.append(pl.BlockSpec((1, tn), lambda i, j: (0, j)))
        out_specs += [pl.BlockSpec((tm, tn), lambda i, j: (i, j)), pl.BlockSpec((tm, LANES), lambda i, j: (i, 0))]
        out_shape += [jax.ShapeDtypeStruct((n, m), BF16), jax.ShapeDtypeStruct((n, LANES), F32)]
        args.append(g_next.reshape(1, m))
        scratch = [pltpu.VMEM((tm, 1), F32)]
    out = pl.pallas_call(
        functools.partial(_ple_kernel, emit=emit, d=m),
        grid=(n // tm, m // tn),
        in_specs=in_specs, out_specs=out_specs, out_shape=out_shape, scratch_shapes=scratch,
        compiler_params=_params("parallel", "arbitrary"),
        name="ple",
    )(*args)
    return out if emit else out[0]


def _sg_kernel(u_ref, v_ref, gv_ref, ws_ref, bias_ref, o_ref, *, heads):
    tb = u_ref.shape[0]
    t, hd = SG_CHUNK, SG_HEAD_DIM
    v = v_ref[...].astype(F32)
    vn = (v * lax.rsqrt(jnp.mean(v * v, axis=-1, keepdims=True) + EPS) * gv_ref[...]).astype(BF16)
    causal = lax.broadcasted_iota(jnp.int32, (t, t), 0) >= lax.broadcasted_iota(jnp.int32, (t, t), 1)
    for h in range(heads):
        cols = slice(h * hd, (h + 1) * hd)
        wc = jnp.where(causal, ws_ref[h], 0.0).astype(BF16)
        bias = bias_ref[:, cols]
        for c in range(tb // t):
            rows = slice(c * t, (c + 1) * t)
            mixed = _dot(wc, vn[rows, cols]) + bias
            o_ref[rows, cols] = (u_ref[rows, cols].astype(F32) * mixed).astype(o_ref.dtype)


def spatial_gate(z, g_v, w_s, b_s, tb=512):
    n, w2 = z.shape
    w = w2 // 2
    heads = w // SG_HEAD_DIM
    tb = _tile(n, tb, SG_CHUNK)
    bias = jnp.repeat(b_s.T, SG_HEAD_DIM, axis=1)
    return pl.pallas_call(
        functools.partial(_sg_kernel, heads=heads),
        grid=(n // tb,),
        in_specs=[pl.BlockSpec((tb, w), lambda i: (i, 0)),
                  pl.BlockSpec((tb, w), lambda i: (i, 1)),
                  pl.BlockSpec((1, w), lambda i: (0, 0)),
                  pl.BlockSpec((heads, SG_CHUNK, SG_CHUNK), lambda i: (0, 0, 0)),
                  pl.BlockSpec((SG_CHUNK, w), lambda i: (0, 0))],
        out_specs=pl.BlockSpec((tb, w), lambda i: (i, 0)),
        out_shape=jax.ShapeDtypeStruct((n, w), BF16),
        compiler_params=_params("parallel"),
        name="spatial_gate",
    )(z, z, g_v.reshape(1, w), w_s, bias)


def _gate_proj_kernel(x_ref, rstd_ref, w_ref, o_ref, *, rank):
    a = _dot(x_ref[...], w_ref[...]) * rstd_ref[...]
    o_ref[...] = jnp.where(lax.broadcasted_iota(jnp.int32, a.shape, 1) < rank, a, 0.0)


def gla_gate_proj(xg, rstd, w_in, col0, rank, tm=1024):
    n, d = xg.shape
    assert col0 % LANES == 0
    tm = _tile(n, tm, SUBLANES)
    return pl.pallas_call(
        functools.partial(_gate_proj_kernel, rank=rank),
        grid=(n // tm,),
        in_specs=[pl.BlockSpec((tm, d), lambda i: (i, 0)),
                  pl.BlockSpec((tm, LANES), lambda i: (i, 0)),
                  pl.BlockSpec((d, LANES), lambda i: (0, col0 // LANES))],
        out_specs=pl.BlockSpec((tm, LANES), lambda i: (i, 0)),
        out_shape=jax.ShapeDtypeStruct((n, LANES), F32),
        compiler_params=_params("parallel"),
        name="gla_gate_proj",
    )(xg, rstd, w_in)


def _chunk_log2_decay(a, wa_ref, ba_ref):
    c = a.shape[0]
    z2 = (_dot(a.astype(BF16), wa_ref[...]) + ba_ref[...]) * LOG2E
    g2 = (jnp.minimum(z2, 0.0) - jnp.log2(1.0 + jnp.exp2(-jnp.abs(z2)))) * np.float32(1.0 / GLA_GATE_NORM)
    tri = (lax.broadcasted_iota(jnp.int32, (c, c), 0) >= lax.broadcasted_iota(jnp.int32, (c, c), 1)).astype(BF16)
    g_hi, g_lo = _split_bf16(g2)
    return _dot(tri, g_hi) + _dot(tri, g_lo)


def _gla_kernel(q_ref, k_ref, v_ref, r_ref, a_ref, wa_ref, ba_ref, go_ref, o_ref, s_ref, sc_ref, *, scale):
    c, dk = q_ref.shape
    sub, half = GLA_SUB, SUBLANES

    @pl.when(pl.program_id(2) == 0)
    def _():
        s_ref[...] = jnp.zeros_like(s_ref)

    b = _chunk_log2_decay(a_ref[...], wa_ref, ba_ref)
    q = q_ref[...].astype(F32) * scale
    k = k_ref[...].astype(F32)
    v = v_ref[...]
    o = _dot((q * jnp.exp2(b)).astype(BF16), s_ref[...].astype(BF16))

    lane = lax.broadcasted_iota(jnp.int32, (half, c), 1)
    t_loc = lax.broadcasted_iota(jnp.int32, (half, 1), 0)
    for i in range(c // sub):
        i0 = i * sub
        qi, bi, ki = q[i0:i0 + sub], b[i0:i0 + sub], k[i0:i0 + sub]
        a_top = jnp.zeros((half, c), F32)
        a_bot = jnp.zeros((half, c), F32)
        if i > 0:
            ref = b[i0 - 1:i0]
            qt = (qi * jnp.exp2(bi - ref)).astype(BF16)
            kt = (k * jnp.exp2(jnp.minimum(ref - b, 0.0))).astype(BF16)
            a_off = lax.dot_general(qt, kt, (((1,), (1,)), ((), ())), preferred_element_type=F32)
            a_top = jnp.where(lane < i0, a_off[:half], 0.0)
            a_bot = jnp.where(lane < i0, a_off[half:], 0.0)
        q_top, b_top, q_bot, b_bot = qi[:half], bi[:half], qi[half:], bi[half:]
        for s in range(sub):
            ks, bs = ki[s:s + 1], bi[s:s + 1]
            col = jnp.sum(q_bot * ks * jnp.exp2(jnp.minimum(b_bot - bs, 0.0)), axis=-1, keepdims=True)
            if s >= half:
                col = jnp.where(t_loc >= s - half, col, 0.0)
            a_bot = a_bot + jnp.where(lane == i0 + s, col, 0.0)
            if s < half:
                col = jnp.sum(q_top * ks * jnp.exp2(jnp.minimum(b_top - bs, 0.0)), axis=-1, keepdims=True)
                a_top = a_top + jnp.where(lane == i0 + s, jnp.where(t_loc >= s, col, 0.0), 0.0)
        sc_ref[i0:i0 + half, :] = a_top
        sc_ref[i0 + half:i0 + sub, :] = a_bot
    o = o + _dot(sc_ref[...].astype(BF16), v)

    b_last = b[c - 1:c]
    k_dec = (k * jnp.exp2(b_last - b)).astype(BF16)
    upd = lax.dot_general(k_dec, v, (((0,), (0,)), ((), ())), preferred_element_type=F32)
    decay_col = jnp.transpose(jnp.broadcast_to(jnp.exp2(b_last), (LANES, dk)))[:, 0:1]
    s_ref[...] = decay_col * s_ref[...] + upd

    r = r_ref[...].astype(F32)
    on = o * lax.rsqrt(jnp.mean(o * o, axis=-1, keepdims=True) + EPS) * go_ref[...]
    o_ref[...] = (on * _silu(r)).astype(o_ref.dtype)


def gla(qkvr, a_low, w_alpha, b_alpha, g_o, bsz, seq):
    n = qkvr.shape[0]
    rank, kd = w_alpha.shape
    vd = (qkvr.shape[1] - 2 * kd) // 2
    dk, dv = kd // GLA_HEADS, vd // GLA_HEADS
    c = _tile(seq, GLA_KERNEL_CHUNK, GLA_SUB)
    nc = seq // c
    assert kd % dk == 0 and (2 * kd) % dv == 0 and (2 * kd + vd) % dv == 0
    k_blk, v_blk, r_blk = kd // dk, (2 * kd) // dv, (2 * kd + vd) // dv
    w_alpha = jnp.pad(w_alpha, ((0, LANES - rank), (0, 0))).astype(BF16)

    def row(b, h, j):
        return b * nc + j

    return pl.pallas_call(
        functools.partial(_gla_kernel, scale=np.float32(dk ** -0.5)),
        grid=(bsz, GLA_HEADS, nc),
        in_specs=[pl.BlockSpec((c, dk), lambda b, h, j: (row(b, h, j), h)),
                  pl.BlockSpec((c, dk), lambda b, h, j: (row(b, h, j), k_blk + h)),
                  pl.BlockSpec((c, dv), lambda b, h, j: (row(b, h, j), v_blk + h)),
                  pl.BlockSpec((c, dv), lambda b, h, j: (row(b, h, j), r_blk + h)),
                  pl.BlockSpec((c, LANES), lambda b, h, j: (row(b, h, j), 0)),
                  pl.BlockSpec((LANES, dk), lambda b, h, j: (0, h)),
                  pl.BlockSpec((1, dk), lambda b, h, j: (0, h)),
                  pl.BlockSpec((1, dv), lambda b, h, j: (0, h))],
        out_specs=pl.BlockSpec((c, dv), lambda b, h, j: (row(b, h, j), h)),
        out_shape=jax.ShapeDtypeStruct((n, vd), BF16),
        scratch_shapes=[pltpu.VMEM((dk, dv), F32), pltpu.VMEM((c, c), F32)],
        compiler_params=_params("parallel", "parallel", "arbitrary"),
        name="gla",
    )(qkvr, qkvr, qkvr, qkvr, a_low, w_alpha, b_alpha.reshape(1, kd), g_o.reshape(1, vd))


def _pack_bf16_pair(lo, hi):
    lo_bits = lax.bitcast_convert_type(lo.astype(BF16).astype(F32), U32)
    hi_bits = lax.bitcast_convert_type(hi.astype(BF16).astype(F32), U32)
    return (hi_bits & U32(0xFFFF0000)) | (lo_bits >> 16)


def _unpack_bf16_pair(words):
    lo = lax.bitcast_convert_type(words << 16, F32).astype(BF16)
    hi = lax.bitcast_convert_type(words & U32(0xFFFF0000), F32).astype(BF16)
    return lo, hi


def _router_kernel(x_ref, g_ref, wr_ref, h_ref, gates_ref, sel_ref, *, n_experts):
    x = x_ref[...]
    h = x * lax.rsqrt(jnp.mean(x * x, axis=-1, keepdims=True) + EPS) * g_ref[...]
    half = h.shape[1] // 2
    h_ref[...] = _pack_bf16_pair(h[:, :half], h[:, half:])
    logits = jnp.dot(h, wr_ref[...], precision=lax.Precision.HIGHEST, preferred_element_type=F32)
    lane = lax.broadcasted_iota(jnp.int32, logits.shape, 1)
    neg = -jnp.inf
    logits = jnp.where(lane < n_experts, logits, neg)
    m1 = jnp.max(logits, axis=-1, keepdims=True)
    i1 = jnp.min(jnp.where(logits == m1, lane, LANES), axis=-1, keepdims=True)
    is1 = lane == i1
    rest = jnp.where(is1, neg, logits)
    m2 = jnp.max(rest, axis=-1, keepdims=True)
    i2 = jnp.min(jnp.where(rest == m2, lane, LANES), axis=-1, keepdims=True)
    is2 = lane == i2
    e2 = jnp.exp(m2 - m1)
    den = 1.0 + e2
    gates_ref[...] = jnp.where(is1, 1.0 / den, 0.0) + jnp.where(is2, e2 / den, 0.0)
    sel_ref[...] = jnp.where(is1 | is2, 1, 0).astype(jnp.int32)


def moe_router(x, g, w_router, tm=256):
    n, d = x.shape
    e = w_router.shape[1]
    tm = _tile(n, tm, SUBLANES)
    wr = jnp.pad(w_router, ((0, 0), (0, LANES - e)))
    return pl.pallas_call(
        functools.partial(_router_kernel, n_experts=e),
        grid=(n // tm,),
        in_specs=[pl.BlockSpec((tm, d), lambda i: (i, 0)),
                  pl.BlockSpec((1, d), lambda i: (0, 0)),
                  pl.BlockSpec((d, LANES), lambda i: (0, 0))],
        out_specs=[pl.BlockSpec((tm, d // 2), lambda i: (i, 0)),
                   pl.BlockSpec((tm, LANES), lambda i: (i, 0)),
                   pl.BlockSpec((tm, LANES), lambda i: (i, 0))],
        out_shape=[jax.ShapeDtypeStruct((n, d // 2), U32),
                   jax.ShapeDtypeStruct((n, LANES), F32),
                   jax.ShapeDtypeStruct((n, LANES), jnp.int32)],
        compiler_params=_params("parallel"),
        name="moe_router",
    )(x, g.reshape(1, d), wr)


def _row_copy(src_ref, src_row, dst_ref, dst_row, sem):
    return pltpu.make_async_copy(src_ref.at[pl.ds(src_row, 1)], dst_ref.at[pl.ds(dst_row, 1)], sem)


def _dispatch_kernel(dest_ref, h_ref, xs_in_ref, xs_ref, sem):
    del xs_in_ref
    tm = h_ref.shape[0]

    def start(r, carry):
        for kk in range(TOP_K):
            _row_copy(h_ref, r, xs_ref, dest_ref[0, 0, TOP_K * r + kk], sem).start()
        return carry

    lax.fori_loop(0, tm, start, 0)

    def wait(r, carry):
        for kk in range(TOP_K):
            _row_copy(h_ref, 0, xs_ref, 0, sem).wait()
        return carry

    lax.fori_loop(0, tm, wait, 0)


def moe_dispatch(h, dest, n_rows, tm=512):
    n, d = h.shape
    tm = _tile(n, tm, SUBLANES)
    dest3 = dest.reshape(n // tm, 1, TOP_K * tm)
    zeros = jnp.zeros((n_rows, d), h.dtype)
    return pl.pallas_call(
        _dispatch_kernel,
        grid=(n // tm,),
        in_specs=[pl.BlockSpec((1, 1, TOP_K * tm), lambda i: (i, 0, 0), memory_space=pltpu.SMEM),
                  pl.BlockSpec((tm, d), lambda i: (i, 0)),
                  pl.BlockSpec(memory_space=pl.ANY)],
        out_specs=pl.BlockSpec(memory_space=pl.ANY),
        out_shape=jax.ShapeDtypeStruct((n_rows, d), h.dtype),
        scratch_shapes=[pltpu.SemaphoreType.DMA(())],
        input_output_aliases={2: 0},
        compiler_params=_params("arbitrary"),
        name="moe_dispatch",
    )(dest3, h, zeros)


def _expert_changed(te_ref, i):
    return (i == 0) | (te_ref[i] != te_ref[jnp.maximum(i - 1, 0)])


def _moe_up_kernel(te_ref, nt_ref, x_ref, wg_ref, wu_ref, o_ref, wgb_ref, wub_ref):
    i = pl.program_id(1)

    @pl.when(_expert_changed(te_ref, i))
    def _():
        wgb_ref[...] = wg_ref[0].astype(BF16)
        wub_ref[...] = wu_ref[0].astype(BF16)

    half = x_ref.shape[1]

    @pl.when(i < nt_ref[0])
    def _():
        for rows in _row_blocks(o_ref.shape[0]):
            lo, hi = _unpack_bf16_pair(x_ref[rows, :])
            gate = _dot(lo, wgb_ref[:half, :]) + _dot(hi, wgb_ref[half:, :])
            up = _dot(lo, wub_ref[:half, :]) + _dot(hi, wub_ref[half:, :])
            o_ref[rows, :] = (_silu(gate) * up).astype(o_ref.dtype)

    @pl.when(i >= nt_ref[0])
    def _():
        o_ref[...] = jnp.zeros_like(o_ref)


def moe_up(xs, w_gu, tile_expert, n_tiles, tn=512):
    rows, half = xs.shape
    d = 2 * half
    f = w_gu.shape[2] // 2
    tm, tn = MOE_ROW_TILE, _tile(f, tn)
    nj = f // tn
    grid_spec = pltpu.PrefetchScalarGridSpec(
        num_scalar_prefetch=2,
        grid=(nj, rows // tm),
        in_specs=[pl.BlockSpec((tm, half), lambda j, i, te, nt: (i, 0)),
                  pl.BlockSpec((1, d, tn), lambda j, i, te, nt: (te[i], 0, j)),
                  pl.BlockSpec((1, d, tn), lambda j, i, te, nt: (te[i], 0, j + nj))],
        out_specs=pl.BlockSpec((tm, tn), lambda j, i, te, nt: (i, j)),
        scratch_shapes=[pltpu.VMEM((d, tn), BF16), pltpu.VMEM((d, tn), BF16)],
    )
    return pl.pallas_call(
        _moe_up_kernel,
        grid_spec=grid_spec,
        out_shape=jax.ShapeDtypeStruct((rows, f), BF16),
        compiler_params=_params("arbitrary", "arbitrary"),
        name="moe_up",
    )(tile_expert, n_tiles, xs, w_gu, w_gu)


def _moe_down_kernel(te_ref, nt_ref, a_ref, w_ref, o_ref, wb_ref):
    i = pl.program_id(1)

    @pl.when(_expert_changed(te_ref, i))
    def _():
        wb_ref[...] = w_ref[0].astype(BF16)

    @pl.when(i < nt_ref[0])
    def _():
        for rows in _row_blocks(o_ref.shape[0]):
            o_ref[rows, :] = _dot(a_ref[rows, :], wb_ref[...])

    @pl.when(i >= nt_ref[0])
    def _():
        o_ref[...] = jnp.zeros_like(o_ref)


def moe_down(a, w_down, tile_expert, n_tiles, tn=1024):
    rows, f = a.shape
    d = w_down.shape[2]
    tm, tn = MOE_ROW_TILE, _tile(d, tn)
    grid_spec = pltpu.PrefetchScalarGridSpec(
        num_scalar_prefetch=2,
        grid=(d // tn, rows // tm),
        in_specs=[pl.BlockSpec((tm, f), lambda j, i, te, nt: (i, 0)),
                  pl.BlockSpec((1, f, tn), lambda j, i, te, nt: (te[i], 0, j))],
        out_specs=pl.BlockSpec((tm, tn), lambda j, i, te, nt: (i, j)),
        scratch_shapes=[pltpu.VMEM((f, tn), BF16)],
    )
    return pl.pallas_call(
        _moe_down_kernel,
        grid_spec=grid_spec,
        out_shape=jax.ShapeDtypeStruct((rows, d), F32),
        compiler_params=_params("arbitrary", "arbitrary"),
        name="moe_down",
    )(tile_expert, n_tiles, a, w_down)


def _combine_kernel(dest_ref, x_ref, w_ref, y_ref, g_ref, o_ref, xg_ref, rstd_ref, buf_ref, sem):
    tm = x_ref.shape[0]

    def start(r, carry):
        for kk in range(TOP_K):
            _row_copy(y_ref, dest_ref[0, 0, TOP_K * r + kk], buf_ref.at[kk], r, sem).start()
        return carry

    lax.fori_loop(0, tm, start, 0)

    def wait(r, carry):
        for kk in range(TOP_K):
            _row_copy(y_ref, 0, buf_ref.at[kk], 0, sem).wait()
        return carry

    lax.fori_loop(0, tm, wait, 0)
    w = w_ref[...]
    x_new = x_ref[...] + (w[:, 0:1] * buf_ref[0] + w[:, 1:2] * buf_ref[1])
    o_ref[...] = x_new
    xg_ref[...] = (x_new * g_ref[...]).astype(xg_ref.dtype)
    rstd_ref[...] = jnp.broadcast_to(lax.rsqrt(jnp.mean(x_new * x_new, axis=-1, keepdims=True) + EPS), rstd_ref.shape)


def moe_combine(x, y, dest, w, g_next, tm=256):
    n, d = x.shape
    tm = _tile(n, tm, SUBLANES)
    dest3 = dest.reshape(n // tm, 1, TOP_K * tm)
    return pl.pallas_call(
        _combine_kernel,
        grid=(n // tm,),
        in_specs=[pl.BlockSpec((1, 1, TOP_K * tm), lambda i: (i, 0, 0), memory_space=pltpu.SMEM),
                  pl.BlockSpec((tm, d), lambda i: (i, 0)),
                  pl.BlockSpec((tm, LANES), lambda i: (i, 0)),
                  pl.BlockSpec(memory_space=pl.ANY),
                  pl.BlockSpec((1, d), lambda i: (0, 0))],
        out_specs=[pl.BlockSpec((tm, d), lambda i: (i, 0)),
                   pl.BlockSpec((tm, d), lambda i: (i, 0)),
                   pl.BlockSpec((tm, LANES), lambda i: (i, 0))],
        out_shape=[jax.ShapeDtypeStruct((n, d), F32),
                   jax.ShapeDtypeStruct((n, d), BF16),
                   jax.ShapeDtypeStruct((n, LANES), F32)],
        scratch_shapes=[pltpu.VMEM((TOP_K, tm, d), F32), pltpu.SemaphoreType.DMA(())],
        compiler_params=_params("arbitrary"),
        name="moe_combine",
    )(dest3, x, w, y, g_next.reshape(1, d))


def moe_layer(x, g, w_router, w_gu, w_down, g_next):
    n, d = x.shape
    n_exp = w_router.shape[1]
    tm = MOE_ROW_TILE
    h, gates, sel = moe_router(x, g, w_router)
    gates, sel = gates[:, :n_exp], sel[:, :n_exp]

    csum = jnp.cumsum(sel, axis=0)
    counts = csum[-1]
    group = ((counts + tm - 1) // tm) * tm
    ends = jnp.cumsum(group)
    slot = (ends - group)[None, :] + csum - 1
    big = jnp.iinfo(jnp.int32).max
    d_lo = jnp.min(jnp.where(sel > 0, slot, big), axis=1)
    d_hi = jnp.max(jnp.where(sel > 0, slot, -1), axis=1)
    w_lo = jnp.sum(jnp.where((sel > 0) & (slot == d_lo[:, None]), gates, 0.0), axis=1)
    w_hi = jnp.sum(jnp.where((sel > 0) & (slot == d_hi[:, None]), gates, 0.0), axis=1)
    dest = jnp.stack([d_lo, d_hi], axis=1).astype(jnp.int32)
    w = jnp.pad(jnp.stack([w_lo, w_hi], axis=1), ((0, 0), (0, LANES - TOP_K)))

    n_rows = TOP_K * n + n_exp * tm
    n_row_tiles = n_rows // tm
    tile_start = jnp.arange(n_row_tiles, dtype=jnp.int32) * tm
    tile_expert = jnp.minimum(jnp.sum((ends[None, :] <= tile_start[:, None]).astype(jnp.int32), axis=1), n_exp - 1)
    n_tiles = (ends[-1:] // tm).astype(jnp.int32)

    xs = moe_dispatch(h, dest, n_rows)
    a = moe_up(xs, w_gu, tile_expert, n_tiles)
    y = moe_down(a, w_down, tile_expert, n_tiles)
    return moe_combine(x, y, dest, w, g_next)


def kernel(x, p, mix_norm, ffn_norm, ple_norm, ple_w_gate, ple_w_proj, sg_w_in, sg_g_v, sg_w_s, sg_b_s, sg_w_out,
           ffn_w_gu, ffn_w_down, gla_w_in, gla_w_alpha, gla_b_alpha, gla_g_o, gla_w_out, moe_w_router, moe_w_gu,
           moe_w_down, final_norm):
    bsz, seq, d = x.shape
    n = bsz * seq
    depth = p.shape[0]
    x = x.reshape(n, d)
    p = p.reshape(depth, n, -1).astype(BF16)
    bf = lambda w: w.astype(BF16)
    ple_w_gate, ple_w_proj = bf(ple_w_gate), bf(ple_w_proj)

    xg, rstd = norm_inputs(x, mix_norm[0])
    for i in range(depth):
        j = i // 2
        g_after = None if i == depth - 1 else mix_norm[i + 1]
        if i % 2 == 0:
            z = matmul(xg, rstd, bf(sg_w_in[j]), sg_w_in.shape[2], act="gelu")
            gated = spatial_gate(z, sg_g_v[j], sg_w_s[j], sg_b_s[j])
            x, xg, rstd = matmul_residual(gated, bf(sg_w_out[j]), x, ffn_norm[i])
            a = swiglu_up(xg, rstd, bf(ffn_w_gu[j]))
            x, xg, rstd = matmul_residual(a, bf(ffn_w_down[j]), x, ple_norm[i], tn=1024, tk=2048)
        else:
            w_in = bf(gla_w_in[j])
            n_main = w_in.shape[1] - GLA_GATE_RANK
            qkvr = matmul(xg, rstd, w_in, n_main)
            a_low = gla_gate_proj(xg, rstd, w_in, n_main, GLA_GATE_RANK)
            og = gla(qkvr, a_low, gla_w_alpha[j], gla_b_alpha[j], gla_g_o[j], bsz, seq)
            x = matmul_residual(og, bf(gla_w_out[j]), x)
            x, xg, rstd = moe_layer(x, ffn_norm[i], moe_w_router[j], moe_w_gu[j], moe_w_down[j], ple_norm[i])
        out = ple(xg, rstd, ple_w_gate, ple_w_proj, i, p[i], x, g_after)
        x, xg, rstd = out if g_after is not None else (out, None, None)
    return rmsnorm(x, final_norm).reshape(bsz, seq, d)
```

```python
import functools

import jax
import jax.numpy as jnp
import numpy as np
from jax import lax
from jax.experimental import pallas as pl
from jax.experimental.pallas import tpu as pltpu

EPS = 1e-6
SG_CHUNK = 128
SG_HEAD_DIM = 128
GLA_HEADS = 4
GLA_GATE_RANK = 16
GLA_GATE_NORM = 16.0
TOP_K = 2

V7X_VMEM_LIMIT_BYTES = 56 * 1024 * 1024
LANES = 128
SUBLANES = 8
ROW_BLOCK = 128
GLA_KERNEL_CHUNK = 128
GLA_SUB = 2 * SUBLANES
MOE_ROW_TILE = 512
LOG2E = np.float32(np.log2(np.e))

F32 = jnp.float32
BF16 = jnp.bfloat16
U32 = jnp.uint32


def _tile(dim, pref, align=LANES):
    if dim <= pref:
        return dim
    t = (pref // align) * align
    while t > align and dim % t:
        t -= align
    assert dim % t == 0, (dim, pref)
    return t


def _row_blocks(tm):
    rb = min(ROW_BLOCK, tm)
    assert tm % rb == 0
    return [pl.ds(r0, rb) for r0 in range(0, tm, rb)]


def _params(*sem):
    return pltpu.CompilerParams(dimension_semantics=sem, vmem_limit_bytes=V7X_VMEM_LIMIT_BYTES)


def _sigmoid(x):
    return 1.0 / (1.0 + jnp.exp(-x))


def _silu(x):
    return x * _sigmoid(x)


def _dot(a, b):
    return jnp.dot(a, b, preferred_element_type=F32)


def _split_bf16(x):
    hi = x.astype(BF16)
    return hi, (x - hi.astype(F32)).astype(BF16)


def _lane_tiles(r, width):
    return r if width == LANES else jnp.concatenate([r] * (width // LANES), axis=1)


def _emit_norm_inputs(x_new, rows, g_ref, xg_ref, ss_ref, j):
    xg_ref[rows, :] = (x_new * g_ref[...]).astype(xg_ref.dtype)
    part = jnp.sum(x_new * x_new, axis=-1, keepdims=True)
    ss_ref[rows, :] = jnp.where(j == 0, part, ss_ref[rows, :] + part)


def _emit_rstd(ss_ref, rstd_ref, j, nj, d):
    @pl.when(j == nj - 1)
    def _():
        rstd_ref[...] = jnp.broadcast_to(lax.rsqrt(ss_ref[...] * np.float32(1.0 / d) + EPS), rstd_ref.shape)


def _norm_inputs_kernel(x_ref, g_ref, xg_ref, rstd_ref):
    x = x_ref[...]
    xg_ref[...] = (x * g_ref[...]).astype(xg_ref.dtype)
    rstd_ref[...] = jnp.broadcast_to(lax.rsqrt(jnp.mean(x * x, axis=-1, keepdims=True) + EPS), rstd_ref.shape)


def norm_inputs(x, g):
    n, d = x.shape
    tm = _tile(n, 512, SUBLANES)
    return pl.pallas_call(
        _norm_inputs_kernel,
        grid=(n // tm,),
        in_specs=[pl.BlockSpec((tm, d), lambda i: (i, 0)), pl.BlockSpec((1, d), lambda i: (0, 0))],
        out_specs=[pl.BlockSpec((tm, d), lambda i: (i, 0)), pl.BlockSpec((tm, LANES), lambda i: (i, 0))],
        out_shape=[jax.ShapeDtypeStruct((n, d), BF16), jax.ShapeDtypeStruct((n, LANES), F32)],
        compiler_params=_params("parallel"),
        name="norm_inputs",
    )(x, g.reshape(1, d))


def _rmsnorm_kernel(x_ref, g_ref, o_ref):
    x = x_ref[...]
    o_ref[...] = x * lax.rsqrt(jnp.mean(x * x, axis=-1, keepdims=True) + EPS) * g_ref[...]


def rmsnorm(x, g):
    n, d = x.shape
    tm = _tile(n, 512, SUBLANES)
    return pl.pallas_call(
        _rmsnorm_kernel,
        grid=(n // tm,),
        in_specs=[pl.BlockSpec((tm, d), lambda i: (i, 0)), pl.BlockSpec((1, d), lambda i: (0, 0))],
        out_specs=pl.BlockSpec((tm, d), lambda i: (i, 0)),
        out_shape=jax.ShapeDtypeStruct((n, d), F32),
        compiler_params=_params("parallel"),
        name="rmsnorm",
    )(x, g.reshape(1, d))


def _mm_kernel(x_ref, rstd_ref, w_ref, o_ref, *, act):
    tm, tn = o_ref.shape
    for rows in _row_blocks(tm):
        acc = _dot(x_ref[rows, :], w_ref[...]) * _lane_tiles(rstd_ref[rows, :], tn)
        if act == "gelu":
            acc = 0.5 * acc * (1.0 + lax.erf(acc * np.float32(np.sqrt(0.5))))
        o_ref[rows, :] = acc.astype(o_ref.dtype)


def matmul(xg, rstd, w, n_out, act=None, tm=1024, tn=1024):
    n, k = xg.shape
    tm, tn = _tile(n, tm, SUBLANES), _tile(n_out, tn)
    return pl.pallas_call(
        functools.partial(_mm_kernel, act=act),
        grid=(n // tm, n_out // tn),
        in_specs=[pl.BlockSpec((tm, k), lambda i, j: (i, 0)),
                  pl.BlockSpec((tm, LANES), lambda i, j: (i, 0)),
                  pl.BlockSpec((k, tn), lambda i, j: (0, j))],
        out_specs=pl.BlockSpec((tm, tn), lambda i, j: (i, j)),
        out_shape=jax.ShapeDtypeStruct((n, n_out), BF16),
        compiler_params=_params("parallel", "parallel"),
        name="matmul_" + str(act),
    )(xg, rstd, w)


def _mm_res_kernel(x_ref, w_ref, r_ref, *rest, emit, nk, d):
    if emit:
        g_ref, o_ref, xg_ref, rstd_ref, ss_ref = rest
    else:
        (o_ref,) = rest
    j, k = pl.program_id(1), pl.program_id(2)

    def step(first, last):
        for rows in _row_blocks(o_ref.shape[0]):
            base = r_ref[rows, :] if first else o_ref[rows, :]
            x_new = base + _dot(x_ref[rows, :], w_ref[...])
            o_ref[rows, :] = x_new
            if last and emit:
                _emit_norm_inputs(x_new, rows, g_ref, xg_ref, ss_ref, j)
        if last and emit:
            _emit_rstd(ss_ref, rstd_ref, j, pl.num_programs(1), d)

    if nk == 1:
        step(True, True)
    else:
        pl.when(k == 0)(lambda: step(True, False))
        pl.when((k > 0) & (k < nk - 1))(lambda: step(False, False))
        pl.when(k == nk - 1)(lambda: step(False, True))


def matmul_residual(x, w, res, g_next=None, tm=1024, tn=512, tk=4096):
    n, k = x.shape
    m = w.shape[1]
    tm, tn, tk = _tile(n, tm, SUBLANES), _tile(m, tn), _tile(k, tk)
    nk = k // tk
    emit = g_next is not None
    in_specs = [pl.BlockSpec((tm, tk), lambda i, j, kk: (i, kk)),
                pl.BlockSpec((tk, tn), lambda i, j, kk: (kk, j)),
                pl.BlockSpec((tm, tn), lambda i, j, kk: (i, j))]
    out_specs = [pl.BlockSpec((tm, tn), lambda i, j, kk: (i, j))]
    out_shape = [jax.ShapeDtypeStruct((n, m), F32)]
    args = [x, w, res]
    scratch = []
    if emit:
        in_specs.append(pl.BlockSpec((1, tn), lambda i, j, kk: (0, j)))
        out_specs += [pl.BlockSpec((tm, tn), lambda i, j, kk: (i, j)),
                      pl.BlockSpec((tm, LANES), lambda i, j, kk: (i, 0))]
        out_shape += [jax.ShapeDtypeStruct((n, m), BF16), jax.ShapeDtypeStruct((n, LANES), F32)]
        args.append(g_next.reshape(1, m))
        scratch = [pltpu.VMEM((tm, 1), F32)]
    out = pl.pallas_call(
        functools.partial(_mm_res_kernel, emit=emit, nk=nk, d=m),
        grid=(n // tm, m // tn, nk),
        in_specs=in_specs, out_specs=out_specs, out_shape=out_shape, scratch_shapes=scratch,
        compiler_params=_params("parallel", "arbitrary", "arbitrary"),
        name="matmul_residual",
    )(*args)
    return out if emit else out[0]


def _swiglu_kernel(x_ref, rstd_ref, wg_ref, wu_ref, o_ref):
    tm, tn = o_ref.shape
    for rows in _row_blocks(tm):
        x = x_ref[rows, :]
        scale = _lane_tiles(rstd_ref[rows, :], tn)
        o_ref[rows, :] = (_silu(_dot(x, wg_ref[...]) * scale) * (_dot(x, wu_ref[...]) * scale)).astype(o_ref.dtype)


def swiglu_up(xg, rstd, w_gu, tm=1024, tn=512):
    n, k = xg.shape
    f = w_gu.shape[1] // 2
    tm, tn = _tile(n, tm, SUBLANES), _tile(f, tn)
    nj = f // tn
    return pl.pallas_call(
        _swiglu_kernel,
        grid=(n // tm, nj),
        in_specs=[pl.BlockSpec((tm, k), lambda i, j: (i, 0)),
                  pl.BlockSpec((tm, LANES), lambda i, j: (i, 0)),
                  pl.BlockSpec((k, tn), lambda i, j: (0, j)),
                  pl.BlockSpec((k, tn), lambda i, j: (0, j + nj))],
        out_specs=pl.BlockSpec((tm, tn), lambda i, j: (i, j)),
        out_shape=jax.ShapeDtypeStruct((n, f), BF16),
        compiler_params=_params("parallel", "parallel"),
        name="swiglu_up",
    )(xg, rstd, w_gu, w_gu)


def _ple_kernel(xg_ref, rstd_ref, wg_ref, p_ref, wp_ref, x_ref, *rest, emit, d):
    if emit:
        g_ref, o_ref, xg_out_ref, rstd_out_ref, ss_ref = rest
    else:
        (o_ref,) = rest
    tm, tn = o_ref.shape
    j = pl.program_id(1)
    for rows in _row_blocks(tm):
        gate = _sigmoid(_dot(xg_ref[rows, :], wg_ref[0]) * _lane_tiles(rstd_ref[rows, :], tn))
        x_new = x_ref[rows, :] + gate * _dot(p_ref[rows, :], wp_ref[0])
        o_ref[rows, :] = x_new
        if emit:
            _emit_norm_inputs(x_new, rows, g_ref, xg_out_ref, ss_ref, j)
    if emit:
        _emit_rstd(ss_ref, rstd_out_ref, j, pl.num_programs(1), d)


def ple(xg, rstd, w_gate, w_proj, layer, p, x, g_next=None, tm=1024, tn=512):
    n, d = xg.shape
    pd = p.shape[1]
    m = w_gate.shape[2]
    tm, tn = _tile(n, tm, SUBLANES), _tile(m, tn)
    emit = g_next is not None
    in_specs = [pl.BlockSpec((tm, d), lambda i, j: (i, 0)),
                pl.BlockSpec((tm, LANES), lambda i, j: (i, 0)),
                pl.BlockSpec((1, d, tn), lambda i, j: (layer, 0, j)),
                pl.BlockSpec((tm, pd), lambda i, j: (i, 0)),
                pl.BlockSpec((1, pd, tn), lambda i, j: (layer, 0, j)),
                pl.BlockSpec((tm, tn), lambda i, j: (i, j))]
    out_specs = [pl.BlockSpec((tm, tn), lambda i, j: (i, j))]
    out_shape = [jax.ShapeDtypeStruct((n, m), F32)]
    args = [xg, rstd, w_gate, p, w_proj, x]
    scratch = []
    if emit:
        in_specs.append(pl.BlockSpec((1, tn), lambda i, j: (0, j)))
        out_specs += [pl.BlockSpec((tm, tn), lambda i, j: (i, j)), pl.BlockSpec((tm, LANES), lambda i, j: (i, 0))]
        out_shape += [jax.ShapeDtypeStruct((n, m), BF16), jax.ShapeDtypeStruct((n, LANES), F32)]
        args.append(g_next.reshape(1, m))
        scratch = [pltpu.VMEM((tm, 1), F32)]
    out = pl.pallas_call(
        functools.partial(_ple_kernel, emit=emit, d=m),
        grid=(n // tm, m // tn),
        in_specs=in_specs, out_specs=out_specs, out_shape=out_shape, scratch_shapes=scratch,
        compiler_params=_params("parallel", "arbitrary"),
        name="ple",
    )(*args)
    return out if emit else out[0]


def _sg_kernel(u_ref, v_ref, gv_ref, ws_ref, bias_ref, o_ref, *, heads):
    tb = u_ref.shape[0]
    t, hd = SG_CHUNK, SG_HEAD_DIM
    v = v_ref[...].astype(F32)
    vn = (v * lax.rsqrt(jnp.mean(v * v, axis=-1, keepdims=True) + EPS) * gv_ref[...]).astype(BF16)
    causal = lax.broadcasted_iota(jnp.int32, (t, t), 0) >= lax.broadcasted_iota(jnp.int32, (t, t), 1)
    for h in range(heads):
        cols = slice(h * hd, (h + 1) * hd)
        wc = jnp.where(causal, ws_ref[h], 0.0).astype(BF16)
        bias = bias_ref[:, cols]
        for c in range(tb // t):
            rows = slice(c * t, (c + 1) * t)
            mixed = _dot(wc, vn[rows, cols]) + bias
            o_ref[rows, cols] = (u_ref[rows, cols].astype(F32) * mixed).astype(o_ref.dtype)


def spatial_gate(z, g_v, w_s, b_s, tb=512):
    n, w2 = z.shape
    w = w2 // 2
    heads = w // SG_HEAD_DIM
    tb = _tile(n, tb, SG_CHUNK)
    bias = jnp.repeat(b_s.T, SG_HEAD_DIM, axis=1)
    return pl.pallas_call(
        functools.partial(_sg_kernel, heads=heads),
        grid=(n // tb,),
        in_specs=[pl.BlockSpec((tb, w), lambda i: (i, 0)),
                  pl.BlockSpec((tb, w), lambda i: (i, 1)),
                  pl.BlockSpec((1, w), lambda i: (0, 0)),
                  pl.BlockSpec((heads, SG_CHUNK, SG_CHUNK), lambda i: (0, 0, 0)),
                  pl.BlockSpec((SG_CHUNK, w), lambda i: (0, 0))],
        out_specs=pl.BlockSpec((tb, w), lambda i: (i, 0)),
        out_shape=jax.ShapeDtypeStruct((n, w), BF16),
        compiler_params=_params("parallel"),
        name="spatial_gate",
    )(z, z, g_v.reshape(1, w), w_s, bias)


def _gate_proj_kernel(x_ref, rstd_ref, w_ref, o_ref, *, rank):
    a = _dot(x_ref[...], w_ref[...]) * rstd_ref[...]
    o_ref[...] = jnp.where(lax.broadcasted_iota(jnp.int32, a.shape, 1) < rank, a, 0.0)


def gla_gate_proj(xg, rstd, w_in, col0, rank, tm=1024):
    n, d = xg.shape
    assert col0 % LANES == 0
    tm = _tile(n, tm, SUBLANES)
    return pl.pallas_call(
        functools.partial(_gate_proj_kernel, rank=rank),
        grid=(n // tm,),
        in_specs=[pl.BlockSpec((tm, d), lambda i: (i, 0)),
                  pl.BlockSpec((tm, LANES), lambda i: (i, 0)),
                  pl.BlockSpec((d, LANES), lambda i: (0, col0 // LANES))],
        out_specs=pl.BlockSpec((tm, LANES), lambda i: (i, 0)),
        out_shape=jax.ShapeDtypeStruct((n, LANES), F32),
        compiler_params=_params("parallel"),
        name="gla_gate_proj",
    )(xg, rstd, w_in)


def _chunk_log2_decay(a, wa_ref, ba_ref):
    c = a.shape[0]
    z2 = (_dot(a.astype(BF16), wa_ref[...]) + ba_ref[...]) * LOG2E
    g2 = (jnp.minimum(z2, 0.0) - jnp.log2(1.0 + jnp.exp2(-jnp.abs(z2)))) * np.float32(1.0 / GLA_GATE_NORM)
    tri = (lax.broadcasted_iota(jnp.int32, (c, c), 0) >= lax.broadcasted_iota(jnp.int32, (c, c), 1)).astype(BF16)
    g_hi, g_lo = _split_bf16(g2)
    return _dot(tri, g_hi) + _dot(tri, g_lo)


def _gla_kernel(q_ref, k_ref, v_ref, r_ref, a_ref, wa_ref, ba_ref, go_ref, o_ref, s_ref, sc_ref, *, scale):
    c, dk = q_ref.shape
    sub, half = GLA_SUB, SUBLANES

    @pl.when(pl.program_id(2) == 0)
    def _():
        s_ref[...] = jnp.zeros_like(s_ref)

    b = _chunk_log2_decay(a_ref[...], wa_ref, ba_ref)
    q = q_ref[...].astype(F32) * scale
    k = k_ref[...].astype(F32)
    v = v_ref[...]
    o = _dot((q * jnp.exp2(b)).astype(BF16), s_ref[...].astype(BF16))

    lane = lax.broadcasted_iota(jnp.int32, (half, c), 1)
    t_loc = lax.broadcasted_iota(jnp.int32, (half, 1), 0)
    for i in range(c // sub):
        i0 = i * sub
        qi, bi, ki = q[i0:i0 + sub], b[i0:i0 + sub], k[i0:i0 + sub]
        a_top = jnp.zeros((half, c), F32)
        a_bot = jnp.zeros((half, c), F32)
        if i > 0:
            ref = b[i0 - 1:i0]
            qt = (qi * jnp.exp2(bi - ref)).astype(BF16)
            kt = (k * jnp.exp2(jnp.minimum(ref - b, 0.0))).astype(BF16)
            a_off = lax.dot_general(qt, kt, (((1,), (1,)), ((), ())), preferred_element_type=F32)
            a_top = jnp.where(lane < i0, a_off[:half], 0.0)
            a_bot = jnp.where(lane < i0, a_off[half:], 0.0)
        q_top, b_top, q_bot, b_bot = qi[:half], bi[:half], qi[half:], bi[half:]
        for s in range(sub):
            ks, bs = ki[s:s + 1], bi[s:s + 1]
            col = jnp.sum(q_bot * ks * jnp.exp2(jnp.minimum(b_bot - bs, 0.0)), axis=-1, keepdims=True)
            if s >= half:
                col = jnp.where(t_loc >= s - half, col, 0.0)
            a_bot = a_bot + jnp.where(lane == i0 + s, col, 0.0)
            if s < half:
                col = jnp.sum(q_top * ks * jnp.exp2(jnp.minimum(b_top - bs, 0.0)), axis=-1, keepdims=True)
                a_top = a_top + jnp.where(lane == i0 + s, jnp.where(t_loc >= s, col, 0.0), 0.0)
        sc_ref[i0:i0 + half, :] = a_top
        sc_ref[i0 + half:i0 + sub, :] = a_bot
    o = o + _dot(sc_ref[...].astype(BF16), v)

    b_last = b[c - 1:c]
    k_dec = (k * jnp.exp2(b_last - b)).astype(BF16)
    upd = lax.dot_general(k_dec, v, (((0,), (0,)), ((), ())), preferred_element_type=F32)
    decay_col = jnp.transpose(jnp.broadcast_to(jnp.exp2(b_last), (LANES, dk)))[:, 0:1]
    s_ref[...] = decay_col * s_ref[...] + upd

    r = r_ref[...].astype(F32)
    on = o * lax.rsqrt(jnp.mean(o * o, axis=-1, keepdims=True) + EPS) * go_ref[...]
    o_ref[...] = (on * _silu(r)).astype(o_ref.dtype)


def gla(qkvr, a_low, w_alpha, b_alpha, g_o, bsz, seq):
    n = qkvr.shape[0]
    rank, kd = w_alpha.shape
    vd = (qkvr.shape[1] - 2 * kd) // 2
    dk, dv = kd // GLA_HEADS, vd // GLA_HEADS
    c = _tile(seq, GLA_KERNEL_CHUNK, GLA_SUB)
    nc = seq // c
    assert kd % dk == 0 and (2 * kd) % dv == 0 and (2 * kd + vd) % dv == 0
    k_blk, v_blk, r_blk = kd // dk, (2 * kd) // dv, (2 * kd + vd) // dv
    w_alpha = jnp.pad(w_alpha, ((0, LANES - rank), (0, 0))).astype(BF16)

    def row(b, h, j):
        return b * nc + j

    return pl.pallas_call(
        functools.partial(_gla_kernel, scale=np.float32(dk ** -0.5)),
        grid=(bsz, GLA_HEADS, nc),
        in_specs=[pl.BlockSpec((c, dk), lambda b, h, j: (row(b, h, j), h)),
                  pl.BlockSpec((c, dk), lambda b, h, j: (row(b, h, j), k_blk + h)),
                  pl.BlockSpec((c, dv), lambda b, h, j: (row(b, h, j), v_blk + h)),
                  pl.BlockSpec((c, dv), lambda b, h, j: (row(b, h, j), r_blk + h)),
                  pl.BlockSpec((c, LANES), lambda b, h, j: (row(b, h, j), 0)),
                  pl.BlockSpec((LANES, dk), lambda b, h, j: (0, h)),
                  pl.BlockSpec((1, dk), lambda b, h, j: (0, h)),
                  pl.BlockSpec((1, dv), lambda b, h, j: (0, h))],
        out_specs=pl.BlockSpec((c, dv), lambda b, h, j: (row(b, h, j), h)),
        out_shape=jax.ShapeDtypeStruct((n, vd), BF16),
        scratch_shapes=[pltpu.VMEM((dk, dv), F32), pltpu.VMEM((c, c), F32)],
        compiler_params=_params("parallel", "parallel", "arbitrary"),
        name="gla",
    )(qkvr, qkvr, qkvr, qkvr, a_low, w_alpha, b_alpha.reshape(1, kd), g_o.reshape(1, vd))


def _pack_bf16_pair(lo, hi):
    lo_bits = lax.bitcast_convert_type(lo.astype(BF16).astype(F32), U32)
    hi_bits = lax.bitcast_convert_type(hi.astype(BF16).astype(F32), U32)
    return (hi_bits & U32(0xFFFF0000)) | (lo_bits >> 16)


def _unpack_bf16_pair(words):
    lo = lax.bitcast_convert_type(words << 16, F32).astype(BF16)
    hi = lax.bitcast_convert_type(words & U32(0xFFFF0000), F32).astype(BF16)
    return lo, hi


def _router_kernel(x_ref, g_ref, wr_hi_ref, wr_lo_ref, h_ref, gates_ref, sel_ref, *, n_experts):
    x = x_ref[...]
    h = x * lax.rsqrt(jnp.mean(x * x, axis=-1, keepdims=True) + EPS) * g_ref[...]
    half = h.shape[1] // 2
    h_ref[...] = _pack_bf16_pair(h[:, :half], h[:, half:])
    h_hi, h_lo = _split_bf16(h)
    logits = _dot(h_hi, wr_hi_ref[...]) + _dot(h_lo, wr_hi_ref[...]) + _dot(h_hi, wr_lo_ref[...])
    lane = lax.broadcasted_iota(jnp.int32, logits.shape, 1)
    neg = -jnp.inf
    logits = jnp.where(lane < n_experts, logits, neg)
    m1 = jnp.max(logits, axis=-1, keepdims=True)
    i1 = jnp.min(jnp.where(logits == m1, lane, LANES), axis=-1, keepdims=True)
    is1 = lane == i1
    rest = jnp.where(is1, neg, logits)
    m2 = jnp.max(rest, axis=-1, keepdims=True)
    i2 = jnp.min(jnp.where(rest == m2, lane, LANES), axis=-1, keepdims=True)
    is2 = lane == i2
    e2 = jnp.exp(m2 - m1)
    den = 1.0 + e2
    gates_ref[...] = jnp.where(is1, 1.0 / den, 0.0) + jnp.where(is2, e2 / den, 0.0)
    sel_ref[...] = jnp.where(is1 | is2, 1, 0).astype(jnp.int32)


def moe_router(x, g, w_router, tm=256):
    n, d = x.shape
    e = w_router.shape[1]
    tm = _tile(n, tm, SUBLANES)
    wr = jnp.pad(w_router, ((0, 0), (0, LANES - e)))
    wr_hi = wr.astype(BF16)
    wr_lo = (wr - wr_hi.astype(F32)).astype(BF16)
    return pl.pallas_call(
        functools.partial(_router_kernel, n_experts=e),
        grid=(n // tm,),
        in_specs=[pl.BlockSpec((tm, d), lambda i: (i, 0)),
                  pl.BlockSpec((1, d), lambda i: (0, 0)),
                  pl.BlockSpec((d, LANES), lambda i: (0, 0)),
                  pl.BlockSpec((d, LANES), lambda i: (0, 0))],
        out_specs=[pl.BlockSpec((tm, d // 2), lambda i: (i, 0)),
                   pl.BlockSpec((tm, LANES), lambda i: (i, 0)),
                   pl.BlockSpec((tm, LANES), lambda i: (i, 0))],
        out_shape=[jax.ShapeDtypeStruct((n, d // 2), U32),
                   jax.ShapeDtypeStruct((n, LANES), F32),
                   jax.ShapeDtypeStruct((n, LANES), jnp.int32)],
        compiler_params=_params("parallel"),
        name="moe_router",
    )(x, g.reshape(1, d), wr_hi, wr_lo)


def _row_copy(src_ref, src_row, dst_ref, dst_row, sem):
    return pltpu.make_async_copy(src_ref.at[pl.ds(src_row, 1)], dst_ref.at[pl.ds(dst_row, 1)], sem)


def _dispatch_kernel(ends_ref, group_ref, dest_ref, h_ref, xs_ref, zeros_ref, sem, zero_sem):
    tm = h_ref.shape[0]
    tile = zeros_ref.shape[0]

    @pl.when(pl.program_id(0) == 0)
    def _():
        zeros_ref[...] = jnp.zeros_like(zeros_ref)

        n_exp = ends_ref.shape[0]
        n_rows = xs_ref.shape[0]

        def fill(row0):
            return pltpu.make_async_copy(zeros_ref, xs_ref.at[pl.ds(pl.multiple_of(row0, tile), tile)], zero_sem)

        fills = [(group_ref[e] > 0, ends_ref[e] - tile) for e in range(n_exp)]
        fills += [(ends_ref[n_exp - 1] + t * tile < n_rows, ends_ref[n_exp - 1] + t * tile) for t in range(n_exp)]
        for cond, row0 in fills:
            pl.when(cond)(lambda row0=row0: fill(row0).start())
        for cond, row0 in fills:
            pl.when(cond)(lambda row0=row0: fill(row0).wait())

    def start(r, carry):
        for kk in range(TOP_K):
            _row_copy(h_ref, r, xs_ref, dest_ref[0, 0, TOP_K * r + kk], sem).start()
        return carry

    lax.fori_loop(0, tm, start, 0, unroll=8)
    for kk in range(TOP_K):
        pltpu.make_async_copy(h_ref, xs_ref.at[pl.ds(0, tm)], sem).wait()


def moe_dispatch(h, dest, ends, group, n_rows, tm=512):
    n, d = h.shape
    tm = _tile(n, tm, SUBLANES)
    dest3 = dest.reshape(n // tm, 1, TOP_K * tm)
    grid_spec = pltpu.PrefetchScalarGridSpec(
        num_scalar_prefetch=2,
        grid=(n // tm,),
        in_specs=[pl.BlockSpec((1, 1, TOP_K * tm), lambda i, en, gr: (i, 0, 0), memory_space=pltpu.SMEM),
                  pl.BlockSpec((tm, d), lambda i, en, gr: (i, 0))],
        out_specs=pl.BlockSpec(memory_space=pl.ANY),
        scratch_shapes=[pltpu.VMEM((MOE_ROW_TILE, d), h.dtype), pltpu.SemaphoreType.DMA(()),
                        pltpu.SemaphoreType.DMA(())],
    )
    return pl.pallas_call(
        _dispatch_kernel,
        grid_spec=grid_spec,
        out_shape=jax.ShapeDtypeStruct((n_rows, d), h.dtype),
        compiler_params=_params("arbitrary"),
        name="moe_dispatch",
    )(ends, group, dest3, h)


def _expert_weights(te_ref, nx_ref, copies, cast):
    j, i = pl.program_id(0), pl.program_id(1)

    def start(e, jj):
        for c in copies(e, jj):
            c.start()

    pl.when((j == 0) & (i == 0))(lambda: start(te_ref[0], 0))

    @pl.when((i == 0) | (te_ref[i] != te_ref[jnp.maximum(i - 1, 0)]))
    def _():
        for c in copies(te_ref[i], j):
            c.wait()
        cast()
        nxt = nx_ref[i]
        pl.when(nxt >= 0)(lambda: start(nxt, j))
        pl.when((nxt < 0) & (j + 1 < pl.num_programs(0)))(lambda: start(te_ref[0], j + 1))


def _moe_up_kernel(te_ref, nx_ref, nt_ref, x_ref, w_ref, o_ref, wf_ref, wgb_ref, wub_ref, sem):
    i = pl.program_id(1)
    tn = o_ref.shape[1]
    nj = pl.num_programs(0)

    def copies(e, jj):
        return [pltpu.make_async_copy(w_ref.at[e, :, pl.ds(pl.multiple_of((jj + part * nj) * tn, LANES), tn)],
                                      wf_ref.at[part], sem.at[part]) for part in range(2)]

    def cast():
        wgb_ref[...] = wf_ref[0].astype(BF16)
        wub_ref[...] = wf_ref[1].astype(BF16)

    _expert_weights(te_ref, nx_ref, copies, cast)
    half = x_ref.shape[1]

    @pl.when(i < nt_ref[0])
    def _():
        for rows in _row_blocks(o_ref.shape[0]):
            lo, hi = _unpack_bf16_pair(x_ref[rows, :])
            gate = _dot(lo, wgb_ref[:half, :]) + _dot(hi, wgb_ref[half:, :])
            up = _dot(lo, wub_ref[:half, :]) + _dot(hi, wub_ref[half:, :])
            o_ref[rows, :] = (_silu(gate) * up).astype(o_ref.dtype)

    @pl.when(i >= nt_ref[0])
    def _():
        o_ref[...] = jnp.zeros_like(o_ref)


def moe_up(xs, w_gu, tile_expert, next_expert, n_tiles, tn=512):
    rows, half = xs.shape
    d = 2 * half
    f = w_gu.shape[2] // 2
    tm, tn = MOE_ROW_TILE, _tile(f, tn)
    nj = f // tn
    grid_spec = pltpu.PrefetchScalarGridSpec(
        num_scalar_prefetch=3,
        grid=(nj, rows // tm),
        in_specs=[pl.BlockSpec((tm, half), lambda j, i, te, nx, nt: (i, 0)),
                  pl.BlockSpec(memory_space=pl.ANY)],
        out_specs=pl.BlockSpec((tm, tn), lambda j, i, te, nx, nt: (i, j)),
        scratch_shapes=[pltpu.VMEM((2, d, tn), F32), pltpu.VMEM((d, tn), BF16), pltpu.VMEM((d, tn), BF16),
                        pltpu.SemaphoreType.DMA((2,))],
    )
    return pl.pallas_call(
        _moe_up_kernel,
        grid_spec=grid_spec,
        out_shape=jax.ShapeDtypeStruct((rows, f), BF16),
        compiler_params=_params("arbitrary", "arbitrary"),
        name="moe_up",
    )(tile_expert, next_expert, n_tiles, xs, w_gu)


def _moe_down_kernel(te_ref, nx_ref, nt_ref, a_ref, w_ref, o_ref, wf_ref, wb_ref, sem):
    i = pl.program_id(1)
    tn = o_ref.shape[1]

    def copies(e, jj):
        return [pltpu.make_async_copy(w_ref.at[e, :, pl.ds(pl.multiple_of(jj * tn, LANES), tn)], wf_ref, sem)]

    def cast():
        wb_ref[...] = wf_ref[...].astype(BF16)

    _expert_weights(te_ref, nx_ref, copies, cast)

    @pl.when(i < nt_ref[0])
    def _():
        for rows in _row_blocks(o_ref.shape[0]):
            o_ref[rows, :] = _dot(a_ref[rows, :], wb_ref[...])

    @pl.when(i >= nt_ref[0])
    def _():
        o_ref[...] = jnp.zeros_like(o_ref)


def moe_down(a, w_down, tile_expert, next_expert, n_tiles, tn=1024):
    rows, f = a.shape
    d = w_down.shape[2]
    tm, tn = MOE_ROW_TILE, _tile(d, tn)
    grid_spec = pltpu.PrefetchScalarGridSpec(
        num_scalar_prefetch=3,
        grid=(d // tn, rows // tm),
        in_specs=[pl.BlockSpec((tm, f), lambda j, i, te, nx, nt: (i, 0)),
                  pl.BlockSpec(memory_space=pl.ANY)],
        out_specs=pl.BlockSpec((tm, tn), lambda j, i, te, nx, nt: (i, j)),
        scratch_shapes=[pltpu.VMEM((f, tn), F32), pltpu.VMEM((f, tn), BF16), pltpu.SemaphoreType.DMA(())],
    )
    return pl.pallas_call(
        _moe_down_kernel,
        grid_spec=grid_spec,
        out_shape=jax.ShapeDtypeStruct((rows, d), F32),
        compiler_params=_params("arbitrary", "arbitrary"),
        name="moe_down",
    )(tile_expert, next_expert, n_tiles, a, w_down)


def _combine_kernel(dest_ref, x_ref, w_ref, y_ref, g_ref, o_ref, xg_ref, rstd_ref, buf_ref, sem):
    tm = x_ref.shape[0]

    def start(r, carry):
        for kk in range(TOP_K):
            _row_copy(y_ref, dest_ref[0, 0, TOP_K * r + kk], buf_ref.at[kk], r, sem).start()
        return carry

    lax.fori_loop(0, tm, start, 0, unroll=8)
    for kk in range(TOP_K):
        pltpu.make_async_copy(y_ref.at[pl.ds(0, tm)], buf_ref.at[kk], sem).wait()
    w = w_ref[...]
    x_new = x_ref[...] + (w[:, 0:1] * buf_ref[0] + w[:, 1:2] * buf_ref[1])
    o_ref[...] = x_new
    xg_ref[...] = (x_new * g_ref[...]).astype(xg_ref.dtype)
    rstd_ref[...] = jnp.broadcast_to(lax.rsqrt(jnp.mean(x_new * x_new, axis=-1, keepdims=True) + EPS), rstd_ref.shape)


def moe_combine(x, y, dest, w, g_next, tm=256):
    n, d = x.shape
    tm = _tile(n, tm, SUBLANES)
    dest3 = dest.reshape(n // tm, 1, TOP_K * tm)
    return pl.pallas_call(
        _combine_kernel,
        grid=(n // tm,),
        in_specs=[pl.BlockSpec((1, 1, TOP_K * tm), lambda i: (i, 0, 0), memory_space=pltpu.SMEM),
                  pl.BlockSpec((tm, d), lambda i: (i, 0)),
                  pl.BlockSpec((tm, LANES), lambda i: (i, 0)),
                  pl.BlockSpec(memory_space=pl.ANY),
                  pl.BlockSpec((1, d), lambda i: (0, 0))],
        out_specs=[pl.BlockSpec((tm, d), lambda i: (i, 0)),
                   pl.BlockSpec((tm, d), lambda i: (i, 0)),
                   pl.BlockSpec((tm, LANES), lambda i: (i, 0))],
        out_shape=[jax.ShapeDtypeStruct((n, d), F32),
                   jax.ShapeDtypeStruct((n, d), BF16),
                   jax.ShapeDtypeStruct((n, LANES), F32)],
        scratch_shapes=[pltpu.VMEM((TOP_K, tm, d), F32), pltpu.SemaphoreType.DMA(())],
        compiler_params=_params("arbitrary"),
        name="moe_combine",
    )(dest3, x, w, y, g_next.reshape(1, d))


def moe_layer(x, g, w_router, w_gu, w_down, g_next):
    n, d = x.shape
    n_exp = w_router.shape[1]
    tm = MOE_ROW_TILE
    h, gates, sel = moe_router(x, g, w_router)
    gates, sel = gates[:, :n_exp], sel[:, :n_exp]

    csum = jnp.cumsum(sel, axis=0)
    counts = csum[-1]
    group = ((counts + tm - 1) // tm) * tm
    ends = jnp.cumsum(group)
    slot = (ends - group)[None, :] + csum - 1
    big = jnp.iinfo(jnp.int32).max
    d_lo = jnp.min(jnp.where(sel > 0, slot, big), axis=1)
    d_hi = jnp.max(jnp.where(sel > 0, slot, -1), axis=1)
    w_lo = jnp.sum(jnp.where((sel > 0) & (slot == d_lo[:, None]), gates, 0.0), axis=1)
    w_hi = jnp.sum(jnp.where((sel > 0) & (slot == d_hi[:, None]), gates, 0.0), axis=1)
    dest = jnp.stack([d_lo, d_hi], axis=1).astype(jnp.int32)
    w = jnp.pad(jnp.stack([w_lo, w_hi], axis=1), ((0, 0), (0, LANES - TOP_K)))

    n_rows = TOP_K * n + n_exp * tm
    n_row_tiles = n_rows // tm
    tile_start = jnp.arange(n_row_tiles, dtype=jnp.int32) * tm
    tile_expert = jnp.minimum(jnp.sum((ends[None, :] <= tile_start[:, None]).astype(jnp.int32), axis=1), n_exp - 1)
    n_tiles = (ends[-1:] // tm).astype(jnp.int32)
    tile_idx = jnp.arange(n_row_tiles, dtype=jnp.int32)
    later_other = (tile_idx[None, :] > tile_idx[:, None]) & (tile_expert[None, :] != tile_expert[:, None])
    first_other = jnp.min(jnp.where(later_other, tile_idx[None, :], n_row_tiles), axis=1)
    expert_there = jnp.sum(jnp.where(tile_idx[None, :] == first_other[:, None], tile_expert[None, :], 0), axis=1)
    next_expert = jnp.where(first_other < n_row_tiles, expert_there, -1).astype(jnp.int32)

    xs = moe_dispatch(h, dest, ends.astype(jnp.int32), group.astype(jnp.int32), n_rows)
    a = moe_up(xs, w_gu, tile_expert, next_expert, n_tiles)
    y = moe_down(a, w_down, tile_expert, next_expert, n_tiles)
    return moe_combine(x, y, dest, w, g_next)


def kernel(x, p, mix_norm, ffn_norm, ple_norm, ple_w_gate, ple_w_proj, sg_w_in, sg_g_v, sg_w_s, sg_b_s, sg_w_out,
           ffn_w_gu, ffn_w_down, gla_w_in, gla_w_alpha, gla_b_alpha, gla_g_o, gla_w_out, moe_w_router, moe_w_gu,
           moe_w_down, final_norm):
    bsz, seq, d = x.shape
    n = bsz * seq
    depth = p.shape[0]
    x = x.reshape(n, d)
    p = p.reshape(depth, n, -1).astype(BF16)
    bf = lambda w: w.astype(BF16)
    ple_w_gate, ple_w_proj = bf(ple_w_gate), bf(ple_w_proj)

    xg, rstd = norm_inputs(x, mix_norm[0])
    for i in range(depth):
        j = i // 2
        g_after = None if i == depth - 1 else mix_norm[i + 1]
        if i % 2 == 0:
            z = matmul(xg, rstd, bf(sg_w_in[j]), sg_w_in.shape[2], act="gelu")
            gated = spatial_gate(z, sg_g_v[j], sg_w_s[j], sg_b_s[j])
            x, xg, rstd = matmul_residual(gated, bf(sg_w_out[j]), x, ffn_norm[i])
            a = swiglu_up(xg, rstd, bf(ffn_w_gu[j]))
            x, xg, rstd = matmul_residual(a, bf(ffn_w_down[j]), x, ple_norm[i], tn=1024, tk=3584)
        else:
            w_in = bf(gla_w_in[j])
            n_main = w_in.shape[1] - GLA_GATE_RANK
            qkvr = matmul(xg, rstd, w_in, n_main)
            a_low = gla_gate_proj(xg, rstd, w_in, n_main, GLA_GATE_RANK)
            og = gla(qkvr, a_low, gla_w_alpha[j], gla_b_alpha[j], gla_g_o[j], bsz, seq)
            x = matmul_residual(og, bf(gla_w_out[j]), x)
            x, xg, rstd = moe_layer(x, ffn_norm[i], moe_w_router[j], moe_w_gu[j], moe_w_down[j], ple_norm[i])
        out = ple(xg, rstd, ple_w_gate, ple_w_proj, i, p[i], x, g_after)
        x, xg, rstd = out if g_after is not None else (out, None, None)
    return rmsnorm(x, final_norm).reshape(bsz, seq, d)
```

```python
import functools

import jax
import jax.numpy as jnp
import numpy as np
from jax import lax
from jax.experimental import pallas as pl
from jax.experimental.pallas import tpu as pltpu

EPS = 1e-6
SG_CHUNK = 128
SG_HEAD_DIM = 128
GLA_HEADS = 4
GLA_GATE_RANK = 16
GLA_GATE_NORM = 16.0
TOP_K = 2

V7X_VMEM_LIMIT_BYTES = 56 * 1024 * 1024
LANES = 128
SUBLANES = 8
ROW_BLOCK = 128
GLA_KERNEL_CHUNK = 128
GLA_SUB = 2 * SUBLANES
MOE_ROW_TILE = 512
LOG2E = np.float32(np.log2(np.e))

F32 = jnp.float32
BF16 = jnp.bfloat16
U32 = jnp.uint32


def _tile(dim, pref, align=LANES):
    if dim <= pref:
        return dim
    t = (pref // align) * align
    while t > align and dim % t:
        t -= align
    assert dim % t == 0, (dim, pref)
    return t


def _row_blocks(tm):
    rb = min(ROW_BLOCK, tm)
    assert tm % rb == 0
    return [pl.ds(r0, rb) for r0 in range(0, tm, rb)]


def _params(*sem):
    return pltpu.CompilerParams(dimension_semantics=sem, vmem_limit_bytes=V7X_VMEM_LIMIT_BYTES)


def _sigmoid(x):
    return 1.0 / (1.0 + jnp.exp(-x))


def _silu(x):
    return x * _sigmoid(x)


def _dot(a, b):
    return jnp.dot(a, b, preferred_element_type=F32)


def _split_bf16(x):
    hi = x.astype(BF16)
    return hi, (x - hi.astype(F32)).astype(BF16)


def _lane_tiles(r, width):
    return r if width == LANES else jnp.concatenate([r] * (width // LANES), axis=1)


def _emit_norm_inputs(x_new, rows, g_ref, xg_ref, ss_ref, j):
    xg_ref[rows, :] = (x_new * g_ref[...]).astype(xg_ref.dtype)
    part = jnp.sum(x_new * x_new, axis=-1, keepdims=True)
    ss_ref[rows, :] = jnp.where(j == 0, part, ss_ref[rows, :] + part)


def _emit_rstd(ss_ref, rstd_ref, j, nj, d):
    @pl.when(j == nj - 1)
    def _():
        rstd_ref[...] = jnp.broadcast_to(lax.rsqrt(ss_ref[...] * np.float32(1.0 / d) + EPS), rstd_ref.shape)


def _norm_inputs_kernel(x_ref, g_ref, xg_ref, rstd_ref):
    x = x_ref[...]
    xg_ref[...] = (x * g_ref[...]).astype(xg_ref.dtype)
    rstd_ref[...] = jnp.broadcast_to(lax.rsqrt(jnp.mean(x * x, axis=-1, keepdims=True) + EPS), rstd_ref.shape)


def norm_inputs(x, g):
    n, d = x.shape
    tm = _tile(n, 512, SUBLANES)
    return pl.pallas_call(
        _norm_inputs_kernel,
        grid=(n // tm,),
        in_specs=[pl.BlockSpec((tm, d), lambda i: (i, 0)), pl.BlockSpec((1, d), lambda i: (0, 0))],
        out_specs=[pl.BlockSpec((tm, d), lambda i: (i, 0)), pl.BlockSpec((tm, LANES), lambda i: (i, 0))],
        out_shape=[jax.ShapeDtypeStruct((n, d), BF16), jax.ShapeDtypeStruct((n, LANES), F32)],
        compiler_params=_params("parallel"),
        name="norm_inputs",
    )(x, g.reshape(1, d))


def _rmsnorm_kernel(x_ref, g_ref, o_ref):
    x = x_ref[...]
    o_ref[...] = x * lax.rsqrt(jnp.mean(x * x, axis=-1, keepdims=True) + EPS) * g_ref[...]


def rmsnorm(x, g):
    n, d = x.shape
    tm = _tile(n, 512, SUBLANES)
    return pl.pallas_call(
        _rmsnorm_kernel,
        grid=(n // tm,),
        in_specs=[pl.BlockSpec((tm, d), lambda i: (i, 0)), pl.BlockSpec((1, d), lambda i: (0, 0))],
        out_specs=pl.BlockSpec((tm, d), lambda i: (i, 0)),
        out_shape=jax.ShapeDtypeStruct((n, d), F32),
        compiler_params=_params("parallel"),
        name="rmsnorm",
    )(x, g.reshape(1, d))


def _mm_kernel(x_ref, rstd_ref, w_ref, o_ref, *, act):
    tm, tn = o_ref.shape
    for rows in _row_blocks(tm):
        acc = _dot(x_ref[rows, :], w_ref[...]) * _lane_tiles(rstd_ref[rows, :], tn)
        if act == "gelu":
            acc = 0.5 * acc * (1.0 + lax.erf(acc * np.float32(np.sqrt(0.5))))
        o_ref[rows, :] = acc.astype(o_ref.dtype)


def matmul(xg, rstd, w, n_out, act=None, tm=1024, tn=1024):
    n, k = xg.shape
    tm, tn = _tile(n, tm, SUBLANES), _tile(n_out, tn)
    return pl.pallas_call(
        functools.partial(_mm_kernel, act=act),
        grid=(n // tm, n_out // tn),
        in_specs=[pl.BlockSpec((tm, k), lambda i, j: (i, 0)),
                  pl.BlockSpec((tm, LANES), lambda i, j: (i, 0)),
                  pl.BlockSpec((k, tn), lambda i, j: (0, j))],
        out_specs=pl.BlockSpec((tm, tn), lambda i, j: (i, j)),
        out_shape=jax.ShapeDtypeStruct((n, n_out), BF16),
        compiler_params=_params("parallel", "parallel"),
        name="matmul_" + str(act),
    )(xg, rstd, w)


def _mm_res_kernel(x_ref, w_ref, r_ref, *rest, emit, nk, d):
    if emit:
        g_ref, o_ref, xg_ref, rstd_ref, ss_ref = rest
    else:
        (o_ref,) = rest
    j, k = pl.program_id(1), pl.program_id(2)

    def step(first, last):
        for rows in _row_blocks(o_ref.shape[0]):
            base = r_ref[rows, :] if first else o_ref[rows, :]
            x_new = base + _dot(x_ref[rows, :], w_ref[...])
            o_ref[rows, :] = x_new
            if last and emit:
                _emit_norm_inputs(x_new, rows, g_ref, xg_ref, ss_ref, j)
        if last and emit:
            _emit_rstd(ss_ref, rstd_ref, j, pl.num_programs(1), d)

    if nk == 1:
        step(True, True)
    else:
        pl.when(k == 0)(lambda: step(True, False))
        pl.when((k > 0) & (k < nk - 1))(lambda: step(False, False))
        pl.when(k == nk - 1)(lambda: step(False, True))


def matmul_residual(x, w, res, g_next=None, tm=1024, tn=512, tk=4096):
    n, k = x.shape
    m = w.shape[1]
    tm, tn, tk = _tile(n, tm, SUBLANES), _tile(m, tn), _tile(k, tk)
    nk = k // tk
    emit = g_next is not None
    in_specs = [pl.BlockSpec((tm, tk), lambda i, j, kk: (i, kk)),
                pl.BlockSpec((tk, tn), lambda i, j, kk: (kk, j)),
                pl.BlockSpec((tm, tn), lambda i, j, kk: (i, j))]
    out_specs = [pl.BlockSpec((tm, tn), lambda i, j, kk: (i, j))]
    out_shape = [jax.ShapeDtypeStruct((n, m), F32)]
    args = [x, w, res]
    scratch = []
    if emit:
        in_specs.append(pl.BlockSpec((1, tn), lambda i, j, kk: (0, j)))
        out_specs += [pl.BlockSpec((tm, tn), lambda i, j, kk: (i, j)),
                      pl.BlockSpec((tm, LANES), lambda i, j, kk: (i, 0))]
        out_shape += [jax.ShapeDtypeStruct((n, m), BF16), jax.ShapeDtypeStruct((n, LANES), F32)]
        args.append(g_next.reshape(1, m))
        scratch = [pltpu.VMEM((tm, 1), F32)]
    out = pl.pallas_call(
        functools.partial(_mm_res_kernel, emit=emit, nk=nk, d=m),
        grid=(n // tm, m // tn, nk),
        in_specs=in_specs, out_specs=out_specs, out_shape=out_shape, scratch_shapes=scratch,
        compiler_params=_params("parallel", "arbitrary", "arbitrary"),
        name="matmul_residual",
    )(*args)
    return out if emit else out[0]


def _swiglu_kernel(x_ref, rstd_ref, wg_ref, wu_ref, o_ref):
    tm, tn = o_ref.shape
    for rows in _row_blocks(tm):
        x = x_ref[rows, :]
        scale = _lane_tiles(rstd_ref[rows, :], tn)
        o_ref[rows, :] = (_silu(_dot(x, wg_ref[...]) * scale) * (_dot(x, wu_ref[...]) * scale)).astype(o_ref.dtype)


def swiglu_up(xg, rstd, w_gu, tm=1024, tn=512):
    n, k = xg.shape
    f = w_gu.shape[1] // 2
    tm, tn = _tile(n, tm, SUBLANES), _tile(f, tn)
    nj = f // tn
    return pl.pallas_call(
        _swiglu_kernel,
        grid=(n // tm, nj),
        in_specs=[pl.BlockSpec((tm, k), lambda i, j: (i, 0)),
                  pl.BlockSpec((tm, LANES), lambda i, j: (i, 0)),
                  pl.BlockSpec((k, tn), lambda i, j: (0, j)),
                  pl.BlockSpec((k, tn), lambda i, j: (0, j + nj))],
        out_specs=pl.BlockSpec((tm, tn), lambda i, j: (i, j)),
        out_shape=jax.ShapeDtypeStruct((n, f), BF16),
        compiler_params=_params("parallel", "parallel"),
        name="swiglu_up",
    )(xg, rstd, w_gu, w_gu)


def _ple_kernel(xg_ref, rstd_ref, wg_ref, p_ref, wp_ref, x_ref, *rest, emit, d):
    if emit:
        g_ref, o_ref, xg_out_ref, rstd_out_ref, ss_ref = rest
    else:
        (o_ref,) = rest
    tm, tn = o_ref.shape
    j = pl.program_id(1)
    for rows in _row_blocks(tm):
        gate = _sigmoid(_dot(xg_ref[rows, :], wg_ref[0]) * _lane_tiles(rstd_ref[rows, :], tn))
        x_new = x_ref[rows, :] + gate * _dot(p_ref[rows, :], wp_ref[0])
        o_ref[rows, :] = x_new
        if emit:
            _emit_norm_inputs(x_new, rows, g_ref, xg_out_ref, ss_ref, j)
    if emit:
        _emit_rstd(ss_ref, rstd_out_ref, j, pl.num_programs(1), d)


def ple(xg, rstd, w_gate, w_proj, layer, p, x, g_next=None, tm=1024, tn=512):
    n, d = xg.shape
    pd = p.shape[1]
    m = w_gate.shape[2]
    tm, tn = _tile(n, tm, SUBLANES), _tile(m, tn)
    emit = g_next is not None
    in_specs = [pl.BlockSpec((tm, d), lambda i, j: (i, 0)),
                pl.BlockSpec((tm, LANES), lambda i, j: (i, 0)),
                pl.BlockSpec((1, d, tn), lambda i, j: (layer, 0, j)),
                pl.BlockSpec((tm, pd), lambda i, j: (i, 0)),
                pl.BlockSpec((1, pd, tn), lambda i, j: (layer, 0, j)),
                pl.BlockSpec((tm, tn), lambda i, j: (i, j))]
    out_specs = [pl.BlockSpec((tm, tn), lambda i, j: (i, j))]
    out_shape = [jax.ShapeDtypeStruct((n, m), F32)]
    args = [xg, rstd, w_gate, p, w_proj, x]
    scratch = []
    if emit:
        in_specs.append(pl.BlockSpec((1, tn), lambda i, j: (0, j)))
        out_specs += [pl.BlockSpec((tm, tn), lambda i, j: (i, j)), pl.BlockSpec((tm, LANES), lambda i, j: (i, 0))]
        out_shape += [jax.ShapeDtypeStruct((n, m), BF16), jax.ShapeDtypeStruct((n, LANES), F32)]
        args.append(g_next.reshape(1, m))
        scratch = [pltpu.VMEM((tm, 1), F32)]
    out = pl.pallas_call(
        functools.partial(_ple_kernel, emit=emit, d=m),
        grid=(n // tm, m // tn),
        in_specs=in_specs, out_specs=out_specs, out_shape=out_shape, scratch_shapes=scratch,
        compiler_params=_params("parallel", "arbitrary"),
        name="ple",
    )(*args)
    return out if emit else out[0]


def _sg_kernel(u_ref, v_ref, gv_ref, ws_ref, bias_ref, o_ref, *, heads):
    tb = u_ref.shape[0]
    t, hd = SG_CHUNK, SG_HEAD_DIM
    v = v_ref[...].astype(F32)
    vn = (v * lax.rsqrt(jnp.mean(v * v, axis=-1, keepdims=True) + EPS) * gv_ref[...]).astype(BF16)
    causal = lax.broadcasted_iota(jnp.int32, (t, t), 0) >= lax.broadcasted_iota(jnp.int32, (t, t), 1)
    for h in range(heads):
        cols = slice(h * hd, (h + 1) * hd)
        wc = jnp.where(causal, ws_ref[h], 0.0).astype(BF16)
        bias = bias_ref[:, cols]
        for c in range(tb // t):
            rows = slice(c * t, (c + 1) * t)
            mixed = _dot(wc, vn[rows, cols]) + bias
            o_ref[rows, cols] = (u_ref[rows, cols].astype(F32) * mixed).astype(o_ref.dtype)


def spatial_gate(z, g_v, w_s, b_s, tb=512):
    n, w2 = z.shape
    w = w2 // 2
    heads = w // SG_HEAD_DIM
    tb = _tile(n, tb, SG_CHUNK)
    bias = jnp.repeat(b_s.T, SG_HEAD_DIM, axis=1)
    return pl.pallas_call(
        functools.partial(_sg_kernel, heads=heads),
        grid=(n // tb,),
        in_specs=[pl.BlockSpec((tb, w), lambda i: (i, 0)),
                  pl.BlockSpec((tb, w), lambda i: (i, 1)),
                  pl.BlockSpec((1, w), lambda i: (0, 0)),
                  pl.BlockSpec((heads, SG_CHUNK, SG_CHUNK), lambda i: (0, 0, 0)),
                  pl.BlockSpec((SG_CHUNK, w), lambda i: (0, 0))],
        out_specs=pl.BlockSpec((tb, w), lambda i: (i, 0)),
        out_shape=jax.ShapeDtypeStruct((n, w), BF16),
        compiler_params=_params("parallel"),
        name="spatial_gate",
    )(z, z, g_v.reshape(1, w), w_s, bias)


def _gate_proj_kernel(x_ref, rstd_ref, w_ref, o_ref, *, rank):
    a = _dot(x_ref[...], w_ref[...]) * rstd_ref[...]
    o_ref[...] = jnp.where(lax.broadcasted_iota(jnp.int32, a.shape, 1) < rank, a, 0.0)


def gla_gate_proj(xg, rstd, w_in, col0, rank, tm=1024):
    n, d = xg.shape
    assert col0 % LANES == 0
    tm = _tile(n, tm, SUBLANES)
    return pl.pallas_call(
        functools.partial(_gate_proj_kernel, rank=rank),
        grid=(n // tm,),
        in_specs=[pl.BlockSpec((tm, d), lambda i: (i, 0)),
                  pl.BlockSpec((tm, LANES), lambda i: (i, 0)),
                  pl.BlockSpec((d, LANES), lambda i: (0, col0 // LANES))],
        out_specs=pl.BlockSpec((tm, LANES), lambda i: (i, 0)),
        out_shape=jax.ShapeDtypeStruct((n, LANES), F32),
        compiler_params=_params("parallel"),
        name="gla_gate_proj",
    )(xg, rstd, w_in)


def _chunk_log2_decay(a, wa_ref, ba_ref):
    c = a.shape[0]
    z2 = (_dot(a.astype(BF16), wa_ref[...]) + ba_ref[...]) * LOG2E
    g2 = (jnp.minimum(z2, 0.0) - jnp.log2(1.0 + jnp.exp2(-jnp.abs(z2)))) * np.float32(1.0 / GLA_GATE_NORM)
    tri = (lax.broadcasted_iota(jnp.int32, (c, c), 0) >= lax.broadcasted_iota(jnp.int32, (c, c), 1)).astype(BF16)
    g_hi, g_lo = _split_bf16(g2)
    return _dot(tri, g_hi) + _dot(tri, g_lo)


def _gla_kernel(q_ref, k_ref, v_ref, r_ref, a_ref, wa_ref, ba_ref, go_ref, o_ref, s_ref, sc_ref, *, scale):
    c, dk = q_ref.shape
    sub, half = GLA_SUB, SUBLANES

    @pl.when(pl.program_id(2) == 0)
    def _():
        s_ref[...] = jnp.zeros_like(s_ref)

    b = _chunk_log2_decay(a_ref[...], wa_ref, ba_ref)
    q = q_ref[...].astype(F32) * scale
    k = k_ref[...].astype(F32)
    v = v_ref[...]
    o = _dot((q * jnp.exp2(b)).astype(BF16), s_ref[...].astype(BF16))

    lane = lax.broadcasted_iota(jnp.int32, (half, c), 1)
    t_loc = lax.broadcasted_iota(jnp.int32, (half, 1), 0)
    for i in range(c // sub):
        i0 = i * sub
        qi, bi, ki = q[i0:i0 + sub], b[i0:i0 + sub], k[i0:i0 + sub]
        a_top = jnp.zeros((half, c), F32)
        a_bot = jnp.zeros((half, c), F32)
        if i > 0:
            ref = b[i0 - 1:i0]
            qt = (qi * jnp.exp2(bi - ref)).astype(BF16)
            kt = (k * jnp.exp2(jnp.minimum(ref - b, 0.0))).astype(BF16)
            a_off = lax.dot_general(qt, kt, (((1,), (1,)), ((), ())), preferred_element_type=F32)
            a_top = jnp.where(lane < i0, a_off[:half], 0.0)
            a_bot = jnp.where(lane < i0, a_off[half:], 0.0)
        q_top, b_top, q_bot, b_bot = qi[:half], bi[:half], qi[half:], bi[half:]
        for s in range(sub):
            ks, bs = ki[s:s + 1], bi[s:s + 1]
            col = jnp.sum(q_bot * ks * jnp.exp2(jnp.minimum(b_bot - bs, 0.0)), axis=-1, keepdims=True)
            if s >= half:
                col = jnp.where(t_loc >= s - half, col, 0.0)
            a_bot = a_bot + jnp.where(lane == i0 + s, col, 0.0)
            if s < half:
                col = jnp.sum(q_top * ks * jnp.exp2(jnp.minimum(b_top - bs, 0.0)), axis=-1, keepdims=True)
                a_top = a_top + jnp.where(lane == i0 + s, jnp.where(t_loc >= s, col, 0.0), 0.0)
        sc_ref[i0:i0 + half, :] = a_top
        sc_ref[i0 + half:i0 + sub, :] = a_bot
    o = o + _dot(sc_ref[...].astype(BF16), v)

    b_last = b[c - 1:c]
    k_dec = (k * jnp.exp2(b_last - b)).astype(BF16)
    upd = lax.dot_general(k_dec, v, (((0,), (0,)), ((), ())), preferred_element_type=F32)
    decay_col = jnp.transpose(jnp.broadcast_to(jnp.exp2(b_last), (LANES, dk)))[:, 0:1]
    s_ref[...] = decay_col * s_ref[...] + upd

    r = r_ref[...].astype(F32)
    on = o * lax.rsqrt(jnp.mean(o * o, axis=-1, keepdims=True) + EPS) * go_ref[...]
    o_ref[...] = (on * _silu(r)).astype(o_ref.dtype)


def gla(qkvr, a_low, w_alpha, b_alpha, g_o, bsz, seq):
    n = qkvr.shape[0]
    rank, kd = w_alpha.shape
    vd = (qkvr.shape[1] - 2 * kd) // 2
    dk, dv = kd // GLA_HEADS, vd // GLA_HEADS
    c = _tile(seq, GLA_KERNEL_CHUNK, GLA_SUB)
    nc = seq // c
    assert kd % dk == 0 and (2 * kd) % dv == 0 and (2 * kd + vd) % dv == 0
    k_blk, v_blk, r_blk = kd // dk, (2 * kd) // dv, (2 * kd + vd) // dv
    w_alpha = jnp.pad(w_alpha, ((0, LANES - rank), (0, 0))).astype(BF16)

    def row(b, h, j):
        return b * nc + j

    return pl.pallas_call(
        functools.partial(_gla_kernel, scale=np.float32(dk ** -0.5)),
        grid=(bsz, GLA_HEADS, nc),
        in_specs=[pl.BlockSpec((c, dk), lambda b, h, j: (row(b, h, j), h)),
                  pl.BlockSpec((c, dk), lambda b, h, j: (row(b, h, j), k_blk + h)),
                  pl.BlockSpec((c, dv), lambda b, h, j: (row(b, h, j), v_blk + h)),
                  pl.BlockSpec((c, dv), lambda b, h, j: (row(b, h, j), r_blk + h)),
                  pl.BlockSpec((c, LANES), lambda b, h, j: (row(b, h, j), 0)),
                  pl.BlockSpec((LANES, dk), lambda b, h, j: (0, h)),
                  pl.BlockSpec((1, dk), lambda b, h, j: (0, h)),
                  pl.BlockSpec((1, dv), lambda b, h, j: (0, h))],
        out_specs=pl.BlockSpec((c, dv), lambda b, h, j: (row(b, h, j), h)),
        out_shape=jax.ShapeDtypeStruct((n, vd), BF16),
        scratch_shapes=[pltpu.VMEM((dk, dv), F32), pltpu.VMEM((c, c), F32)],
        compiler_params=_params("parallel", "parallel", "arbitrary"),
        name="gla",
    )(qkvr, qkvr, qkvr, qkvr, a_low, w_alpha, b_alpha.reshape(1, kd), g_o.reshape(1, vd))


def _pack_bf16_pair(lo, hi):
    lo_bits = lax.bitcast_convert_type(lo.astype(BF16).astype(F32), U32)
    hi_bits = lax.bitcast_convert_type(hi.astype(BF16).astype(F32), U32)
    return (hi_bits & U32(0xFFFF0000)) | (lo_bits >> 16)


def _unpack_bf16_pair(words):
    lo = lax.bitcast_convert_type(words << 16, F32).astype(BF16)
    hi = lax.bitcast_convert_type(words & U32(0xFFFF0000), F32).astype(BF16)
    return lo, hi


def _router_kernel(x_ref, g_ref, wr_hi_ref, wr_lo_ref, h_ref, gates_ref, sel_ref, *, n_experts):
    x = x_ref[...]
    h = x * lax.rsqrt(jnp.mean(x * x, axis=-1, keepdims=True) + EPS) * g_ref[...]
    half = h.shape[1] // 2
    h_ref[...] = _pack_bf16_pair(h[:, :half], h[:, half:])
    h_hi, h_lo = _split_bf16(h)
    logits = _dot(h_hi, wr_hi_ref[...]) + _dot(h_lo, wr_hi_ref[...]) + _dot(h_hi, wr_lo_ref[...])
    lane = lax.broadcasted_iota(jnp.int32, logits.shape, 1)
    neg = -jnp.inf
    logits = jnp.where(lane < n_experts, logits, neg)
    m1 = jnp.max(logits, axis=-1, keepdims=True)
    i1 = jnp.min(jnp.where(logits == m1, lane, LANES), axis=-1, keepdims=True)
    is1 = lane == i1
    rest = jnp.where(is1, neg, logits)
    m2 = jnp.max(rest, axis=-1, keepdims=True)
    i2 = jnp.min(jnp.where(rest == m2, lane, LANES), axis=-1, keepdims=True)
    is2 = lane == i2
    e2 = jnp.exp(m2 - m1)
    den = 1.0 + e2
    gates_ref[...] = jnp.where(is1, 1.0 / den, 0.0) + jnp.where(is2, e2 / den, 0.0)
    sel_ref[...] = jnp.where(is1 | is2, 1, 0).astype(jnp.int32)


def moe_router(x, g, w_router, tm=256):
    n, d = x.shape
    e = w_router.shape[1]
    tm = _tile(n, tm, SUBLANES)
    wr = jnp.pad(w_router, ((0, 0), (0, LANES - e)))
    wr_hi = wr.astype(BF16)
    wr_lo = (wr - wr_hi.astype(F32)).astype(BF16)
    return pl.pallas_call(
        functools.partial(_router_kernel, n_experts=e),
        grid=(n // tm,),
        in_specs=[pl.BlockSpec((tm, d), lambda i: (i, 0)),
                  pl.BlockSpec((1, d), lambda i: (0, 0)),
                  pl.BlockSpec((d, LANES), lambda i: (0, 0)),
                  pl.BlockSpec((d, LANES), lambda i: (0, 0))],
        out_specs=[pl.BlockSpec((tm, d // 2), lambda i: (i, 0)),
                   pl.BlockSpec((tm, LANES), lambda i: (i, 0)),
                   pl.BlockSpec((tm, LANES), lambda i: (i, 0))],
        out_shape=[jax.ShapeDtypeStruct((n, d // 2), U32),
                   jax.ShapeDtypeStruct((n, LANES), F32),
                   jax.ShapeDtypeStruct((n, LANES), jnp.int32)],
        compiler_params=_params("parallel"),
        name="moe_router",
    )(x, g.reshape(1, d), wr_hi, wr_lo)


def _row_copy(src_ref, src_row, dst_ref, dst_row, sem):
    return pltpu.make_async_copy(src_ref.at[pl.ds(src_row, 1)], dst_ref.at[pl.ds(dst_row, 1)], sem)


def _dispatch_kernel(ends_ref, group_ref, dest_ref, h_ref, xs_ref, zeros_ref, sem, zero_sem):
    tm = h_ref.shape[0]
    tile = zeros_ref.shape[0]

    @pl.when(pl.program_id(0) == 0)
    def _():
        zeros_ref[...] = jnp.zeros_like(zeros_ref)

        n_exp = ends_ref.shape[0]
        n_rows = xs_ref.shape[0]

        def fill(row0):
            return pltpu.make_async_copy(zeros_ref, xs_ref.at[pl.ds(pl.multiple_of(row0, tile), tile)], zero_sem)

        fills = [(group_ref[e] > 0, ends_ref[e] - tile) for e in range(n_exp)]
        fills += [(ends_ref[n_exp - 1] + t * tile < n_rows, ends_ref[n_exp - 1] + t * tile) for t in range(n_exp)]
        for cond, row0 in fills:
            pl.when(cond)(lambda row0=row0: fill(row0).start())
        for cond, row0 in fills:
            pl.when(cond)(lambda row0=row0: fill(row0).wait())

    def start(r, carry):
        for kk in range(TOP_K):
            _row_copy(h_ref, r, xs_ref, dest_ref[0, 0, TOP_K * r + kk], sem).start()
        return carry

    lax.fori_loop(0, tm, start, 0, unroll=8)
    for kk in range(TOP_K):
        pltpu.make_async_copy(h_ref, xs_ref.at[pl.ds(0, tm)], sem).wait()


def moe_dispatch(h, dest, ends, group, n_rows, tm=512):
    n, d = h.shape
    tm = _tile(n, tm, SUBLANES)
    dest3 = dest.reshape(n // tm, 1, TOP_K * tm)
    grid_spec = pltpu.PrefetchScalarGridSpec(
        num_scalar_prefetch=2,
        grid=(n // tm,),
        in_specs=[pl.BlockSpec((1, 1, TOP_K * tm), lambda i, en, gr: (i, 0, 0), memory_space=pltpu.SMEM),
                  pl.BlockSpec((tm, d), lambda i, en, gr: (i, 0))],
        out_specs=pl.BlockSpec(memory_space=pl.ANY),
        scratch_shapes=[pltpu.VMEM((MOE_ROW_TILE, d), h.dtype), pltpu.SemaphoreType.DMA(()),
                        pltpu.SemaphoreType.DMA(())],
    )
    return pl.pallas_call(
        _dispatch_kernel,
        grid_spec=grid_spec,
        out_shape=jax.ShapeDtypeStruct((n_rows, d), h.dtype),
        compiler_params=_params("arbitrary"),
        name="moe_dispatch",
    )(ends, group, dest3, h)


def _expert_weights(te_ref, nx_ref, copies, cast):
    j, i = pl.program_id(0), pl.program_id(1)

    def start(e, jj):
        for c in copies(e, jj):
            c.start()

    pl.when((j == 0) & (i == 0))(lambda: start(te_ref[0], 0))

    @pl.when((i == 0) | (te_ref[i] != te_ref[jnp.maximum(i - 1, 0)]))
    def _():
        for c in copies(te_ref[i], j):
            c.wait()
        cast()
        nxt = nx_ref[i]
        pl.when(nxt >= 0)(lambda: start(nxt, j))
        pl.when((nxt < 0) & (j + 1 < pl.num_programs(0)))(lambda: start(te_ref[0], j + 1))


def _moe_up_kernel(te_ref, nx_ref, nt_ref, x_ref, w_ref, o_ref, wf_ref, wgb_ref, wub_ref, sem):
    i = pl.program_id(1)
    tn = o_ref.shape[1]
    nj = pl.num_programs(0)

    def copies(e, jj):
        return [pltpu.make_async_copy(w_ref.at[e, :, pl.ds(pl.multiple_of((jj + part * nj) * tn, LANES), tn)],
                                      wf_ref.at[part], sem.at[part]) for part in range(2)]

    def cast():
        wgb_ref[...] = wf_ref[0].astype(BF16)
        wub_ref[...] = wf_ref[1].astype(BF16)

    _expert_weights(te_ref, nx_ref, copies, cast)
    half = x_ref.shape[1]

    @pl.when(i < nt_ref[0])
    def _():
        for rows in _row_blocks(o_ref.shape[0]):
            lo, hi = _unpack_bf16_pair(x_ref[rows, :])
            gate = _dot(lo, wgb_ref[:half, :]) + _dot(hi, wgb_ref[half:, :])
            up = _dot(lo, wub_ref[:half, :]) + _dot(hi, wub_ref[half:, :])
            o_ref[rows, :] = (_silu(gate) * up).astype(o_ref.dtype)

    @pl.when(i >= nt_ref[0])
    def _():
        o_ref[...] = jnp.zeros_like(o_ref)


def moe_up(xs, w_gu, tile_expert, next_expert, n_tiles, tn=512):
    rows, half = xs.shape
    d = 2 * half
    f = w_gu.shape[2] // 2
    tm, tn = MOE_ROW_TILE, _tile(f, tn)
    nj = f // tn
    grid_spec = pltpu.PrefetchScalarGridSpec(
        num_scalar_prefetch=3,
        grid=(nj, rows // tm),
        in_specs=[pl.BlockSpec((tm, half), lambda j, i, te, nx, nt: (i, 0)),
                  pl.BlockSpec(memory_space=pl.ANY)],
        out_specs=pl.BlockSpec((tm, tn), lambda j, i, te, nx, nt: (i, j)),
        scratch_shapes=[pltpu.VMEM((2, d, tn), F32), pltpu.VMEM((d, tn), BF16), pltpu.VMEM((d, tn), BF16),
                        pltpu.SemaphoreType.DMA((2,))],
    )
    return pl.pallas_call(
        _moe_up_kernel,
        grid_spec=grid_spec,
        out_shape=jax.ShapeDtypeStruct((rows, f), BF16),
        compiler_params=_params("arbitrary", "arbitrary"),
        name="moe_up",
    )(tile_expert, next_expert, n_tiles, xs, w_gu)


def _moe_down_kernel(te_ref, nx_ref, nt_ref, a_ref, w_ref, o_ref, wf_ref, wb_ref, sem):
    i = pl.program_id(1)
    tn = o_ref.shape[1]

    def copies(e, jj):
        return [pltpu.make_async_copy(w_ref.at[e, :, pl.ds(pl.multiple_of(jj * tn, LANES), tn)], wf_ref, sem)]

    def cast():
        wb_ref[...] = wf_ref[...].astype(BF16)

    _expert_weights(te_ref, nx_ref, copies, cast)

    @pl.when(i < nt_ref[0])
    def _():
        for rows in _row_blocks(o_ref.shape[0]):
            o_ref[rows, :] = _dot(a_ref[rows, :], wb_ref[...])

    @pl.when(i >= nt_ref[0])
    def _():
        o_ref[...] = jnp.zeros_like(o_ref)


def moe_down(a, w_down, tile_expert, next_expert, n_tiles, tn=1024):
    rows, f = a.shape
    d = w_down.shape[2]
    tm, tn = MOE_ROW_TILE, _tile(d, tn)
    grid_spec = pltpu.PrefetchScalarGridSpec(
        num_scalar_prefetch=3,
        grid=(d // tn, rows // tm),
        in_specs=[pl.BlockSpec((tm, f), lambda j, i, te, nx, nt: (i, 0)),
                  pl.BlockSpec(memory_space=pl.ANY)],
        out_specs=pl.BlockSpec((tm, tn), lambda j, i, te, nx, nt: (i, j)),
        scratch_shapes=[pltpu.VMEM((f, tn), F32), pltpu.VMEM((f, tn), BF16), pltpu.SemaphoreType.DMA(())],
    )
    return pl.pallas_call(
        _moe_down_kernel,
        grid_spec=grid_spec,
        out_shape=jax.ShapeDtypeStruct((rows, d), F32),
        compiler_params=_params("arbitrary", "arbitrary"),
        name="moe_down",
    )(tile_expert, next_expert, n_tiles, a, w_down)


def _combine_kernel(dest_ref, dest_next_ref, x_ref, w_ref, y_ref, g_ref, o_ref, xg_ref, rstd_ref, buf_ref, sem):
    i, n_steps = pl.program_id(0), pl.num_programs(0)
    tm = x_ref.shape[0]
    slot = lax.rem(i, 2)

    def gather(d_ref, s):
        def start(r, carry):
            for kk in range(TOP_K):
                _row_copy(y_ref, d_ref[0, 0, TOP_K * r + kk], buf_ref.at[s, kk], r, sem.at[s]).start()
            return carry

        lax.fori_loop(0, tm, start, 0, unroll=8)

    pl.when(i == 0)(lambda: gather(dest_ref, 0))
    for kk in range(TOP_K):
        pltpu.make_async_copy(y_ref.at[pl.ds(0, tm)], buf_ref.at[slot, kk], sem.at[slot]).wait()
    pl.when(i + 1 < n_steps)(lambda: gather(dest_next_ref, 1 - slot))
    w = w_ref[...]
    x_new = x_ref[...] + (w[:, 0:1] * buf_ref[slot, 0] + w[:, 1:2] * buf_ref[slot, 1])
    o_ref[...] = x_new
    xg_ref[...] = (x_new * g_ref[...]).astype(xg_ref.dtype)
    rstd_ref[...] = jnp.broadcast_to(lax.rsqrt(jnp.mean(x_new * x_new, axis=-1, keepdims=True) + EPS), rstd_ref.shape)


def moe_combine(x, y, dest, w, g_next, tm=256):
    n, d = x.shape
    tm = _tile(n, tm, SUBLANES)
    n_steps = n // tm
    dest3 = dest.reshape(n_steps, 1, TOP_K * tm)
    return pl.pallas_call(
        _combine_kernel,
        grid=(n_steps,),
        in_specs=[pl.BlockSpec((1, 1, TOP_K * tm), lambda i: (i, 0, 0), memory_space=pltpu.SMEM),
                  pl.BlockSpec((1, 1, TOP_K * tm), lambda i: (jnp.minimum(i + 1, n_steps - 1), 0, 0),
                               memory_space=pltpu.SMEM),
                  pl.BlockSpec((tm, d), lambda i: (i, 0)),
                  pl.BlockSpec((tm, LANES), lambda i: (i, 0)),
                  pl.BlockSpec(memory_space=pl.ANY),
                  pl.BlockSpec((1, d), lambda i: (0, 0))],
        out_specs=[pl.BlockSpec((tm, d), lambda i: (i, 0)),
                   pl.BlockSpec((tm, d), lambda i: (i, 0)),
                   pl.BlockSpec((tm, LANES), lambda i: (i, 0))],
        out_shape=[jax.ShapeDtypeStruct((n, d), F32),
                   jax.ShapeDtypeStruct((n, d), BF16),
                   jax.ShapeDtypeStruct((n, LANES), F32)],
        scratch_shapes=[pltpu.VMEM((2, TOP_K, tm, d), F32), pltpu.SemaphoreType.DMA((2,))],
        compiler_params=_params("arbitrary"),
        name="moe_combine",
    )(dest3, dest3, x, w, y, g_next.reshape(1, d))


def moe_layer(x, g, w_router, w_gu, w_down, g_next):
    n, d = x.shape
    n_exp = w_router.shape[1]
    tm = MOE_ROW_TILE
    h, gates, sel = moe_router(x, g, w_router)
    gates, sel = gates[:, :n_exp], sel[:, :n_exp]

    csum = jnp.cumsum(sel, axis=0)
    counts = csum[-1]
    group = ((counts + tm - 1) // tm) * tm
    ends = jnp.cumsum(group)
    slot = (ends - group)[None, :] + csum - 1
    big = jnp.iinfo(jnp.int32).max
    d_lo = jnp.min(jnp.where(sel > 0, slot, big), axis=1)
    d_hi = jnp.max(jnp.where(sel > 0, slot, -1), axis=1)
    w_lo = jnp.sum(jnp.where((sel > 0) & (slot == d_lo[:, None]), gates, 0.0), axis=1)
    w_hi = jnp.sum(jnp.where((sel > 0) & (slot == d_hi[:, None]), gates, 0.0), axis=1)
    dest = jnp.stack([d_lo, d_hi], axis=1).astype(jnp.int32)
    w = jnp.pad(jnp.stack([w_lo, w_hi], axis=1), ((0, 0), (0, LANES - TOP_K)))

    n_rows = TOP_K * n + n_exp * tm
    n_row_tiles = n_rows // tm
    tile_start = jnp.arange(n_row_tiles, dtype=jnp.int32) * tm
    tile_expert = jnp.minimum(jnp.sum((ends[None, :] <= tile_start[:, None]).astype(jnp.int32), axis=1), n_exp - 1)
    n_tiles = (ends[-1:] // tm).astype(jnp.int32)
    tile_idx = jnp.arange(n_row_tiles, dtype=jnp.int32)
    later_other = (tile_idx[None, :] > tile_idx[:, None]) & (tile_expert[None, :] != tile_expert[:, None])
    first_other = jnp.min(jnp.where(later_other, tile_idx[None, :], n_row_tiles), axis=1)
    expert_there = jnp.sum(jnp.where(tile_idx[None, :] == first_other[:, None], tile_expert[None, :], 0), axis=1)
    next_expert = jnp.where(first_other < n_row_tiles, expert_there, -1).astype(jnp.int32)

    xs = moe_dispatch(h, dest, ends.astype(jnp.int32), group.astype(jnp.int32), n_rows)
    a = moe_up(xs, w_gu, tile_expert, next_expert, n_tiles)
    y = moe_down(a, w_down, tile_expert, next_expert, n_tiles)
    return moe_combine(x, y, dest, w, g_next)


def kernel(x, p, mix_norm, ffn_norm, ple_norm, ple_w_gate, ple_w_proj, sg_w_in, sg_g_v, sg_w_s, sg_b_s, sg_w_out,
           ffn_w_gu, ffn_w_down, gla_w_in, gla_w_alpha, gla_b_alpha, gla_g_o, gla_w_out, moe_w_router, moe_w_gu,
           moe_w_down, final_norm):
    bsz, seq, d = x.shape
    n = bsz * seq
    depth = p.shape[0]
    x = x.reshape(n, d)
    p = p.reshape(depth, n, -1).astype(BF16)
    bf = lambda w: w.astype(BF16)
    ple_w_gate, ple_w_proj = bf(ple_w_gate), bf(ple_w_proj)

    xg, rstd = norm_inputs(x, mix_norm[0])
    for i in range(depth):
        j = i // 2
        g_after = None if i == depth - 1 else mix_norm[i + 1]
        if i % 2 == 0:
            z = matmul(xg, rstd, bf(sg_w_in[j]), sg_w_in.shape[2], act="gelu")
            gated = spatial_gate(z, sg_g_v[j], sg_w_s[j], sg_b_s[j])
            x, xg, rstd = matmul_residual(gated, bf(sg_w_out[j]), x, ffn_norm[i])
            a = swiglu_up(xg, rstd, bf(ffn_w_gu[j]))
            x, xg, rstd = matmul_residual(a, bf(ffn_w_down[j]), x, ple_norm[i], tn=1024, tk=3584)
        else:
            w_in = bf(gla_w_in[j])
            n_main = w_in.shape[1] - GLA_GATE_RANK
            qkvr = matmul(xg, rstd, w_in, n_main)
            a_low = gla_gate_proj(xg, rstd, w_in, n_main, GLA_GATE_RANK)
            og = gla(qkvr, a_low, gla_w_alpha[j], gla_b_alpha[j], gla_g_o[j], bsz, seq)
            x = matmul_residual(og, bf(gla_w_out[j]), x, tn=1024)
            x, xg, rstd = moe_layer(x, ffn_norm[i], moe_w_router[j], moe_w_gu[j], moe_w_down[j], ple_norm[i])
        out = ple(xg, rstd, ple_w_gate, ple_w_proj, i, p[i], x, g_after)
        x, xg, rstd = out if g_after is not None else (out, None, None)
    return rmsnorm(x, final_norm).reshape(bsz, seq, d)
```

```python
import functools

import jax
import jax.numpy as jnp
import numpy as np
from jax import lax
from jax.experimental import pallas as pl
from jax.experimental.pallas import tpu as pltpu

EPS = 1e-6
SG_CHUNK = 128
SG_HEAD_DIM = 128
GLA_HEADS = 4
GLA_GATE_RANK = 16
GLA_GATE_NORM = 16.0
TOP_K = 2

V7X_VMEM_LIMIT_BYTES = 56 * 1024 * 1024
LANES = 128
SUBLANES = 8
ROW_BLOCK = 128
GLA_KERNEL_CHUNK = 128
GLA_SUB = 2 * SUBLANES
MOE_ROW_TILE = 512
LOG2E = np.float32(np.log2(np.e))

F32 = jnp.float32
BF16 = jnp.bfloat16
U32 = jnp.uint32


def _tile(dim, pref, align=LANES):
    if dim <= pref:
        return dim
    t = (pref // align) * align
    while t > align and dim % t:
        t -= align
    assert dim % t == 0, (dim, pref)
    return t


def _row_blocks(tm):
    rb = min(ROW_BLOCK, tm)
    assert tm % rb == 0
    return [pl.ds(r0, rb) for r0 in range(0, tm, rb)]


def _params(*sem):
    return pltpu.CompilerParams(dimension_semantics=sem, vmem_limit_bytes=V7X_VMEM_LIMIT_BYTES)


def _sigmoid(x):
    return 1.0 / (1.0 + jnp.exp(-x))


def _silu(x):
    return x * _sigmoid(x)


def _dot(a, b):
    return jnp.dot(a, b, preferred_element_type=F32)


def _split_bf16(x):
    hi = x.astype(BF16)
    return hi, (x - hi.astype(F32)).astype(BF16)


def _lane_tiles(r, width):
    return r if width == LANES else jnp.concatenate([r] * (width // LANES), axis=1)


def _emit_norm_inputs(x_new, rows, g_ref, xg_ref, ss_ref, j):
    xg_ref[rows, :] = (x_new * g_ref[...]).astype(xg_ref.dtype)
    part = jnp.sum(x_new * x_new, axis=-1, keepdims=True)
    ss_ref[rows, :] = jnp.where(j == 0, part, ss_ref[rows, :] + part)


def _emit_rstd(ss_ref, rstd_ref, j, nj, d):
    @pl.when(j == nj - 1)
    def _():
        rstd_ref[...] = jnp.broadcast_to(lax.rsqrt(ss_ref[...] * np.float32(1.0 / d) + EPS), rstd_ref.shape)


def _norm_inputs_kernel(x_ref, g_ref, xg_ref, rstd_ref):
    x = x_ref[...]
    xg_ref[...] = (x * g_ref[...]).astype(xg_ref.dtype)
    rstd_ref[...] = jnp.broadcast_to(lax.rsqrt(jnp.mean(x * x, axis=-1, keepdims=True) + EPS), rstd_ref.shape)


def norm_inputs(x, g):
    n, d = x.shape
    tm = _tile(n, 512, SUBLANES)
    return pl.pallas_call(
        _norm_inputs_kernel,
        grid=(n // tm,),
        in_specs=[pl.BlockSpec((tm, d), lambda i: (i, 0)), pl.BlockSpec((1, d), lambda i: (0, 0))],
        out_specs=[pl.BlockSpec((tm, d), lambda i: (i, 0)), pl.BlockSpec((tm, LANES), lambda i: (i, 0))],
        out_shape=[jax.ShapeDtypeStruct((n, d), BF16), jax.ShapeDtypeStruct((n, LANES), F32)],
        compiler_params=_params("parallel"),
        name="norm_inputs",
    )(x, g.reshape(1, d))


def _rmsnorm_kernel(x_ref, g_ref, o_ref):
    x = x_ref[...]
    o_ref[...] = x * lax.rsqrt(jnp.mean(x * x, axis=-1, keepdims=True) + EPS) * g_ref[...]


def rmsnorm(x, g):
    n, d = x.shape
    tm = _tile(n, 512, SUBLANES)
    return pl.pallas_call(
        _rmsnorm_kernel,
        grid=(n // tm,),
        in_specs=[pl.BlockSpec((tm, d), lambda i: (i, 0)), pl.BlockSpec((1, d), lambda i: (0, 0))],
        out_specs=pl.BlockSpec((tm, d), lambda i: (i, 0)),
        out_shape=jax.ShapeDtypeStruct((n, d), F32),
        compiler_params=_params("parallel"),
        name="rmsnorm",
    )(x, g.reshape(1, d))


def _mm_kernel(x_ref, rstd_ref, w_ref, o_ref, *, act):
    tm, tn = o_ref.shape
    for rows in _row_blocks(tm):
        acc = _dot(x_ref[rows, :], w_ref[...]) * _lane_tiles(rstd_ref[rows, :], tn)
        if act == "gelu":
            acc = 0.5 * acc * (1.0 + lax.erf(acc * np.float32(np.sqrt(0.5))))
        o_ref[rows, :] = acc.astype(o_ref.dtype)


def matmul(xg, rstd, w, n_out, act=None, tm=1024, tn=1024):
    n, k = xg.shape
    tm, tn = _tile(n, tm, SUBLANES), _tile(n_out, tn)
    return pl.pallas_call(
        functools.partial(_mm_kernel, act=act),
        grid=(n // tm, n_out // tn),
        in_specs=[pl.BlockSpec((tm, k), lambda i, j: (i, 0)),
                  pl.BlockSpec((tm, LANES), lambda i, j: (i, 0)),
                  pl.BlockSpec((k, tn), lambda i, j: (0, j))],
        out_specs=pl.BlockSpec((tm, tn), lambda i, j: (i, j)),
        out_shape=jax.ShapeDtypeStruct((n, n_out), BF16),
        compiler_params=_params("parallel", "parallel"),
        name="matmul_" + str(act),
    )(xg, rstd, w)


def _mm_res_kernel(x_ref, w_ref, r_ref, *rest, emit, nk, d):
    if emit:
        g_ref, o_ref, xg_ref, rstd_ref, ss_ref = rest
    else:
        (o_ref,) = rest
    j, k = pl.program_id(1), pl.program_id(2)

    def step(first, last):
        for rows in _row_blocks(o_ref.shape[0]):
            base = r_ref[rows, :] if first else o_ref[rows, :]
            x_new = base + _dot(x_ref[rows, :], w_ref[...])
            o_ref[rows, :] = x_new
            if last and emit:
                _emit_norm_inputs(x_new, rows, g_ref, xg_ref, ss_ref, j)
        if last and emit:
            _emit_rstd(ss_ref, rstd_ref, j, pl.num_programs(1), d)

    if nk == 1:
        step(True, True)
    else:
        pl.when(k == 0)(lambda: step(True, False))
        pl.when((k > 0) & (k < nk - 1))(lambda: step(False, False))
        pl.when(k == nk - 1)(lambda: step(False, True))


def matmul_residual(x, w, res, g_next=None, tm=1024, tn=512, tk=4096):
    n, k = x.shape
    m = w.shape[1]
    tm, tn, tk = _tile(n, tm, SUBLANES), _tile(m, tn), _tile(k, tk)
    nk = k // tk
    emit = g_next is not None
    in_specs = [pl.BlockSpec((tm, tk), lambda i, j, kk: (i, kk)),
                pl.BlockSpec((tk, tn), lambda i, j, kk: (kk, j)),
                pl.BlockSpec((tm, tn), lambda i, j, kk: (i, j))]
    out_specs = [pl.BlockSpec((tm, tn), lambda i, j, kk: (i, j))]
    out_shape = [jax.ShapeDtypeStruct((n, m), F32)]
    args = [x, w, res]
    scratch = []
    if emit:
        in_specs.append(pl.BlockSpec((1, tn), lambda i, j, kk: (0, j)))
        out_specs += [pl.BlockSpec((tm, tn), lambda i, j, kk: (i, j)),
                      pl.BlockSpec((tm, LANES), lambda i, j, kk: (i, 0))]
        out_shape += [jax.ShapeDtypeStruct((n, m), BF16), jax.ShapeDtypeStruct((n, LANES), F32)]
        args.append(g_next.reshape(1, m))
        scratch = [pltpu.VMEM((tm, 1), F32)]
    out = pl.pallas_call(
        functools.partial(_mm_res_kernel, emit=emit, nk=nk, d=m),
        grid=(n // tm, m // tn, nk),
        in_specs=in_specs, out_specs=out_specs, out_shape=out_shape, scratch_shapes=scratch,
        compiler_params=_params("parallel", "arbitrary", "arbitrary"),
        name="matmul_residual",
    )(*args)
    return out if emit else out[0]


def _swiglu_kernel(x_ref, rstd_ref, wg_ref, wu_ref, o_ref):
    tm, tn = o_ref.shape
    for rows in _row_blocks(tm):
        x = x_ref[rows, :]
        scale = _lane_tiles(rstd_ref[rows, :], tn)
        o_ref[rows, :] = (_silu(_dot(x, wg_ref[...]) * scale) * (_dot(x, wu_ref[...]) * scale)).astype(o_ref.dtype)


def swiglu_up(xg, rstd, w_gu, tm=1024, tn=512):
    n, k = xg.shape
    f = w_gu.shape[1] // 2
    tm, tn = _tile(n, tm, SUBLANES), _tile(f, tn)
    nj = f // tn
    return pl.pallas_call(
        _swiglu_kernel,
        grid=(n // tm, nj),
        in_specs=[pl.BlockSpec((tm, k), lambda i, j: (i, 0)),
                  pl.BlockSpec((tm, LANES), lambda i, j: (i, 0)),
                  pl.BlockSpec((k, tn), lambda i, j: (0, j)),
                  pl.BlockSpec((k, tn), lambda i, j: (0, j + nj))],
        out_specs=pl.BlockSpec((tm, tn), lambda i, j: (i, j)),
        out_shape=jax.ShapeDtypeStruct((n, f), BF16),
        compiler_params=_params("parallel", "parallel"),
        name="swiglu_up",
    )(xg, rstd, w_gu, w_gu)


def _ple_kernel(xg_ref, rstd_ref, wg_ref, p_ref, wp_ref, x_ref, *rest, emit, d):
    if emit:
        g_ref, o_ref, xg_out_ref, rstd_out_ref, ss_ref = rest
    else:
        (o_ref,) = rest
    tm, tn = o_ref.shape
    j = pl.program_id(1)
    for rows in _row_blocks(tm):
        gate = _sigmoid(_dot(xg_ref[rows, :], wg_ref[0]) * _lane_tiles(rstd_ref[rows, :], tn))
        x_new = x_ref[rows, :] + gate * _dot(p_ref[rows, :], wp_ref[0])
        o_ref[rows, :] = x_new
        if emit:
            _emit_norm_inputs(x_new, rows, g_ref, xg_out_ref, ss_ref, j)
    if emit:
        _emit_rstd(ss_ref, rstd_out_ref, j, pl.num_programs(1), d)


def ple(xg, rstd, w_gate, w_proj, layer, p, x, g_next=None, tm=1024, tn=512):
    n, d = xg.shape
    pd = p.shape[1]
    m = w_gate.shape[2]
    tm, tn = _tile(n, tm, SUBLANES), _tile(m, tn)
    emit = g_next is not None
    in_specs = [pl.BlockSpec((tm, d), lambda i, j: (i, 0)),
                pl.BlockSpec((tm, LANES), lambda i, j: (i, 0)),
                pl.BlockSpec((1, d, tn), lambda i, j: (layer, 0, j)),
                pl.BlockSpec((tm, pd), lambda i, j: (i, 0)),
                pl.BlockSpec((1, pd, tn), lambda i, j: (layer, 0, j)),
                pl.BlockSpec((tm, tn), lambda i, j: (i, j))]
    out_specs = [pl.BlockSpec((tm, tn), lambda i, j: (i, j))]
    out_shape = [jax.ShapeDtypeStruct((n, m), F32)]
    args = [xg, rstd, w_gate, p, w_proj, x]
    scratch = []
    if emit:
        in_specs.append(pl.BlockSpec((1, tn), lambda i, j: (0, j)))
        out_specs += [pl.BlockSpec((tm, tn), lambda i, j: (i, j)), pl.BlockSpec((tm, LANES), lambda i, j: (i, 0))]
        out_shape += [jax.ShapeDtypeStruct((n, m), BF16), jax.ShapeDtypeStruct((n, LANES), F32)]
        args.append(g_next.reshape(1, m))
        scratch = [pltpu.VMEM((tm, 1), F32)]
    out = pl.pallas_call(
        functools.partial(_ple_kernel, emit=emit, d=m),
        grid=(n // tm, m // tn),
        in_specs=in_specs, out_specs=out_specs, out_shape=out_shape, scratch_shapes=scratch,
        compiler_params=_params("parallel", "arbitrary"),
        name="ple",
    )(*args)
    return out if emit else out[0]


def _sg_kernel(u_ref, v_ref, gv_ref, ws_ref, bias_ref, o_ref, *, heads):
    tb = u_ref.shape[0]
    t, hd = SG_CHUNK, SG_HEAD_DIM
    v = v_ref[...].astype(F32)
    vn = (v * lax.rsqrt(jnp.mean(v * v, axis=-1, keepdims=True) + EPS) * gv_ref[...]).astype(BF16)
    causal = lax.broadcasted_iota(jnp.int32, (t, t), 0) >= lax.broadcasted_iota(jnp.int32, (t, t), 1)
    for h in range(heads):
        cols = slice(h * hd, (h + 1) * hd)
        wc = jnp.where(causal, ws_ref[h], 0.0).astype(BF16)
        bias = bias_ref[:, cols]
        for c in range(tb // t):
            rows = slice(c * t, (c + 1) * t)
            mixed = _dot(wc, vn[rows, cols]) + bias
            o_ref[rows, cols] = (u_ref[rows, cols].astype(F32) * mixed).astype(o_ref.dtype)


def spatial_gate(z, g_v, w_s, b_s, tb=512):
    n, w2 = z.shape
    w = w2 // 2
    heads = w // SG_HEAD_DIM
    tb = _tile(n, tb, SG_CHUNK)
    bias = jnp.repeat(b_s.T, SG_HEAD_DIM, axis=1)
    return pl.pallas_call(
        functools.partial(_sg_kernel, heads=heads),
        grid=(n // tb,),
        in_specs=[pl.BlockSpec((tb, w), lambda i: (i, 0)),
                  pl.BlockSpec((tb, w), lambda i: (i, 1)),
                  pl.BlockSpec((1, w), lambda i: (0, 0)),
                  pl.BlockSpec((heads, SG_CHUNK, SG_CHUNK), lambda i: (0, 0, 0)),
                  pl.BlockSpec((SG_CHUNK, w), lambda i: (0, 0))],
        out_specs=pl.BlockSpec((tb, w), lambda i: (i, 0)),
        out_shape=jax.ShapeDtypeStruct((n, w), BF16),
        compiler_params=_params("parallel"),
        name="spatial_gate",
    )(z, z, g_v.reshape(1, w), w_s, bias)


def _gate_proj_kernel(x_ref, rstd_ref, w_ref, o_ref, *, rank):
    a = _dot(x_ref[...], w_ref[...]) * rstd_ref[...]
    o_ref[...] = jnp.where(lax.broadcasted_iota(jnp.int32, a.shape, 1) < rank, a, 0.0)


def gla_gate_proj(xg, rstd, w_in, col0, rank, tm=1024):
    n, d = xg.shape
    assert col0 % LANES == 0
    tm = _tile(n, tm, SUBLANES)
    return pl.pallas_call(
        functools.partial(_gate_proj_kernel, rank=rank),
        grid=(n // tm,),
        in_specs=[pl.BlockSpec((tm, d), lambda i: (i, 0)),
                  pl.BlockSpec((tm, LANES), lambda i: (i, 0)),
                  pl.BlockSpec((d, LANES), lambda i: (0, col0 // LANES))],
        out_specs=pl.BlockSpec((tm, LANES), lambda i: (i, 0)),
        out_shape=jax.ShapeDtypeStruct((n, LANES), F32),
        compiler_params=_params("parallel"),
        name="gla_gate_proj",
    )(xg, rstd, w_in)


def _chunk_log2_decay(a, wa_ref, ba_ref):
    c = a.shape[0]
    z2 = (_dot(a.astype(BF16), wa_ref[...]) + ba_ref[...]) * LOG2E
    g2 = (jnp.minimum(z2, 0.0) - jnp.log2(1.0 + jnp.exp2(-jnp.abs(z2)))) * np.float32(1.0 / GLA_GATE_NORM)
    tri = (lax.broadcasted_iota(jnp.int32, (c, c), 0) >= lax.broadcasted_iota(jnp.int32, (c, c), 1)).astype(BF16)
    g_hi, g_lo = _split_bf16(g2)
    return _dot(tri, g_hi) + _dot(tri, g_lo)


def _gla_kernel(q_ref, k_ref, v_ref, r_ref, a_ref, wa_ref, ba_ref, go_ref, o_ref, s_ref, sc_ref, *, scale):
    c, dk = q_ref.shape
    sub, half = GLA_SUB, SUBLANES

    @pl.when(pl.program_id(2) == 0)
    def _():
        s_ref[...] = jnp.zeros_like(s_ref)

    b = _chunk_log2_decay(a_ref[...], wa_ref, ba_ref)
    q = q_ref[...].astype(F32) * scale
    k = k_ref[...].astype(F32)
    v = v_ref[...]
    o = _dot((q * jnp.exp2(b)).astype(BF16), s_ref[...].astype(BF16))

    row_id = lax.broadcasted_iota(jnp.int32, (c, c), 0)
    col_id = lax.broadcasted_iota(jnp.int32, (c, c), 1)
    scores = jnp.zeros((c, c), F32)
    s, log_s = half, 3
    while s < c:
        ref = jnp.concatenate([jnp.broadcast_to(b[p0 + s - 1:p0 + s], (2 * s, dk)) for p0 in range(0, c, 2 * s)],
                              axis=0)
        qt = (q * jnp.exp2(jnp.minimum(b - ref, 0.0))).astype(BF16)
        kt = (k * jnp.exp2(jnp.minimum(ref - b, 0.0))).astype(BF16)
        a_lvl = lax.dot_general(qt, kt, (((1,), (1,)), ((), ())), preferred_element_type=F32)
        same_pair = (row_id >> (log_s + 1)) == (col_id >> (log_s + 1))
        keep = same_pair & (((row_id >> log_s) & 1) == 1) & (((col_id >> log_s) & 1) == 0)
        scores = scores + jnp.where(keep, a_lvl, 0.0)
        s, log_s = 2 * s, log_s + 1
    sc_ref[...] = scores
    lane = lax.broadcasted_iota(jnp.int32, (half, c), 1)
    t_loc = lax.broadcasted_iota(jnp.int32, (half, 1), 0)
    for i in range(c // half):
        i0 = i * half
        qi, bi, ki = q[i0:i0 + half], b[i0:i0 + half], k[i0:i0 + half]
        acc = sc_ref[i0:i0 + half, :]
        for s in range(half):
            col = jnp.sum(qi * ki[s:s + 1] * jnp.exp2(jnp.minimum(bi - bi[s:s + 1], 0.0)), axis=-1, keepdims=True)
            acc = acc + jnp.where(lane == i0 + s, jnp.where(t_loc >= s, col, 0.0), 0.0)
        sc_ref[i0:i0 + half, :] = acc
    o = o + _dot(sc_ref[...].astype(BF16), v)

    b_last = b[c - 1:c]
    k_dec = (k * jnp.exp2(b_last - b)).astype(BF16)
    upd = lax.dot_general(k_dec, v, (((0,), (0,)), ((), ())), preferred_element_type=F32)
    decay_col = jnp.transpose(jnp.broadcast_to(jnp.exp2(b_last), (LANES, dk)))[:, 0:1]
    s_ref[...] = decay_col * s_ref[...] + upd

    r = r_ref[...].astype(F32)
    on = o * lax.rsqrt(jnp.mean(o * o, axis=-1, keepdims=True) + EPS) * go_ref[...]
    o_ref[...] = (on * _silu(r)).astype(o_ref.dtype)


def gla(qkvr, a_low, w_alpha, b_alpha, g_o, bsz, seq):
    n = qkvr.shape[0]
    rank, kd = w_alpha.shape
    vd = (qkvr.shape[1] - 2 * kd) // 2
    dk, dv = kd // GLA_HEADS, vd // GLA_HEADS
    c = _tile(seq, GLA_KERNEL_CHUNK, GLA_SUB)
    nc = seq // c
    assert kd % dk == 0 and (2 * kd) % dv == 0 and (2 * kd + vd) % dv == 0
    k_blk, v_blk, r_blk = kd // dk, (2 * kd) // dv, (2 * kd + vd) // dv
    w_alpha = jnp.pad(w_alpha, ((0, LANES - rank), (0, 0))).astype(BF16)

    def row(b, h, j):
        return b * nc + j

    return pl.pallas_call(
        functools.partial(_gla_kernel, scale=np.float32(dk ** -0.5)),
        grid=(bsz, GLA_HEADS, nc),
        in_specs=[pl.BlockSpec((c, dk), lambda b, h, j: (row(b, h, j), h)),
                  pl.BlockSpec((c, dk), lambda b, h, j: (row(b, h, j), k_blk + h)),
                  pl.BlockSpec((c, dv), lambda b, h, j: (row(b, h, j), v_blk + h)),
                  pl.BlockSpec((c, dv), lambda b, h, j: (row(b, h, j), r_blk + h)),
                  pl.BlockSpec((c, LANES), lambda b, h, j: (row(b, h, j), 0)),
                  pl.BlockSpec((LANES, dk), lambda b, h, j: (0, h)),
                  pl.BlockSpec((1, dk), lambda b, h, j: (0, h)),
                  pl.BlockSpec((1, dv), lambda b, h, j: (0, h))],
        out_specs=pl.BlockSpec((c, dv), lambda b, h, j: (row(b, h, j), h)),
        out_shape=jax.ShapeDtypeStruct((n, vd), BF16),
        scratch_shapes=[pltpu.VMEM((dk, dv), F32), pltpu.VMEM((c, c), F32)],
        compiler_params=_params("parallel", "parallel", "arbitrary"),
        name="gla",
    )(qkvr, qkvr, qkvr, qkvr, a_low, w_alpha, b_alpha.reshape(1, kd), g_o.reshape(1, vd))


def _pack_bf16_pair(lo, hi):
    lo_bits = lax.bitcast_convert_type(lo.astype(BF16).astype(F32), U32)
    hi_bits = lax.bitcast_convert_type(hi.astype(BF16).astype(F32), U32)
    return (hi_bits & U32(0xFFFF0000)) | (lo_bits >> 16)


def _unpack_bf16_pair(words):
    lo = lax.bitcast_convert_type(words << 16, F32).astype(BF16)
    hi = lax.bitcast_convert_type(words & U32(0xFFFF0000), F32).astype(BF16)
    return lo, hi


def _router_kernel(x_ref, g_ref, wr_hi_ref, wr_lo_ref, h_ref, gates_ref, sel_ref, *, n_experts):
    x = x_ref[...]
    h = x * lax.rsqrt(jnp.mean(x * x, axis=-1, keepdims=True) + EPS) * g_ref[...]
    half = h.shape[1] // 2
    h_ref[...] = _pack_bf16_pair(h[:, :half], h[:, half:])
    h_hi, h_lo = _split_bf16(h)
    logits = _dot(h_hi, wr_hi_ref[...]) + _dot(h_lo, wr_hi_ref[...]) + _dot(h_hi, wr_lo_ref[...])
    lane = lax.broadcasted_iota(jnp.int32, logits.shape, 1)
    neg = -jnp.inf
    logits = jnp.where(lane < n_experts, logits, neg)
    m1 = jnp.max(logits, axis=-1, keepdims=True)
    i1 = jnp.min(jnp.where(logits == m1, lane, LANES), axis=-1, keepdims=True)
    is1 = lane == i1
    rest = jnp.where(is1, neg, logits)
    m2 = jnp.max(rest, axis=-1, keepdims=True)
    i2 = jnp.min(jnp.where(rest == m2, lane, LANES), axis=-1, keepdims=True)
    is2 = lane == i2
    e2 = jnp.exp(m2 - m1)
    den = 1.0 + e2
    gates_ref[...] = jnp.where(is1, 1.0 / den, 0.0) + jnp.where(is2, e2 / den, 0.0)
    sel_ref[...] = jnp.where(is1 | is2, 1, 0).astype(jnp.int32)


def moe_router(x, g, w_router, tm=256):
    n, d = x.shape
    e = w_router.shape[1]
    tm = _tile(n, tm, SUBLANES)
    wr = jnp.pad(w_router, ((0, 0), (0, LANES - e)))
    wr_hi = wr.astype(BF16)
    wr_lo = (wr - wr_hi.astype(F32)).astype(BF16)
    return pl.pallas_call(
        functools.partial(_router_kernel, n_experts=e),
        grid=(n // tm,),
        in_specs=[pl.BlockSpec((tm, d), lambda i: (i, 0)),
                  pl.BlockSpec((1, d), lambda i: (0, 0)),
                  pl.BlockSpec((d, LANES), lambda i: (0, 0)),
                  pl.BlockSpec((d, LANES), lambda i: (0, 0))],
        out_specs=[pl.BlockSpec((tm, d // 2), lambda i: (i, 0)),
                   pl.BlockSpec((tm, LANES), lambda i: (i, 0)),
                   pl.BlockSpec((tm, LANES), lambda i: (i, 0))],
        out_shape=[jax.ShapeDtypeStruct((n, d // 2), U32),
                   jax.ShapeDtypeStruct((n, LANES), F32),
                   jax.ShapeDtypeStruct((n, LANES), jnp.int32)],
        compiler_params=_params("parallel"),
        name="moe_router",
    )(x, g.reshape(1, d), wr_hi, wr_lo)


def _row_copy(src_ref, src_row, dst_ref, dst_row, sem):
    return pltpu.make_async_copy(src_ref.at[pl.ds(src_row, 1)], dst_ref.at[pl.ds(dst_row, 1)], sem)


def _dispatch_kernel(ends_ref, group_ref, dest_ref, h_ref, xs_ref, zeros_ref, sem, zero_sem):
    tm = h_ref.shape[0]
    tile = zeros_ref.shape[0]

    @pl.when(pl.program_id(0) == 0)
    def _():
        zeros_ref[...] = jnp.zeros_like(zeros_ref)

        n_exp = ends_ref.shape[0]
        n_rows = xs_ref.shape[0]

        def fill(row0):
            return pltpu.make_async_copy(zeros_ref, xs_ref.at[pl.ds(pl.multiple_of(row0, tile), tile)], zero_sem)

        fills = [(group_ref[e] > 0, ends_ref[e] - tile) for e in range(n_exp)]
        fills += [(ends_ref[n_exp - 1] + t * tile < n_rows, ends_ref[n_exp - 1] + t * tile) for t in range(n_exp)]
        for cond, row0 in fills:
            pl.when(cond)(lambda row0=row0: fill(row0).start())
        for cond, row0 in fills:
            pl.when(cond)(lambda row0=row0: fill(row0).wait())

    def start(r, carry):
        for kk in range(TOP_K):
            _row_copy(h_ref, r, xs_ref, dest_ref[0, 0, TOP_K * r + kk], sem).start()
        return carry

    lax.fori_loop(0, tm, start, 0, unroll=8)
    for kk in range(TOP_K):
        pltpu.make_async_copy(h_ref, xs_ref.at[pl.ds(0, tm)], sem).wait()


def moe_dispatch(h, dest, ends, group, n_rows, tm=512):
    n, d = h.shape
    tm = _tile(n, tm, SUBLANES)
    dest3 = dest.reshape(n // tm, 1, TOP_K * tm)
    grid_spec = pltpu.PrefetchScalarGridSpec(
        num_scalar_prefetch=2,
        grid=(n // tm,),
        in_specs=[pl.BlockSpec((1, 1, TOP_K * tm), lambda i, en, gr: (i, 0, 0), memory_space=pltpu.SMEM),
                  pl.BlockSpec((tm, d), lambda i, en, gr: (i, 0))],
        out_specs=pl.BlockSpec(memory_space=pl.ANY),
        scratch_shapes=[pltpu.VMEM((MOE_ROW_TILE, d), h.dtype), pltpu.SemaphoreType.DMA(()),
                        pltpu.SemaphoreType.DMA(())],
    )
    return pl.pallas_call(
        _dispatch_kernel,
        grid_spec=grid_spec,
        out_shape=jax.ShapeDtypeStruct((n_rows, d), h.dtype),
        compiler_params=_params("arbitrary"),
        name="moe_dispatch",
    )(ends, group, dest3, h)


def _expert_weights(te_ref, nx_ref, copies, cast):
    j, i = pl.program_id(0), pl.program_id(1)

    def start(e, jj):
        for c in copies(e, jj):
            c.start()

    pl.when((j == 0) & (i == 0))(lambda: start(te_ref[0], 0))

    @pl.when((i == 0) | (te_ref[i] != te_ref[jnp.maximum(i - 1, 0)]))
    def _():
        for c in copies(te_ref[i], j):
            c.wait()
        cast()
        nxt = nx_ref[i]
        pl.when(nxt >= 0)(lambda: start(nxt, j))
        pl.when((nxt < 0) & (j + 1 < pl.num_programs(0)))(lambda: start(te_ref[0], j + 1))


def _moe_up_kernel(te_ref, nx_ref, nt_ref, x_ref, w_ref, o_ref, wf_ref, wgb_ref, wub_ref, sem):
    i = pl.program_id(1)
    tn = o_ref.shape[1]
    nj = pl.num_programs(0)

    def copies(e, jj):
        return [pltpu.make_async_copy(w_ref.at[e, :, pl.ds(pl.multiple_of((jj + part * nj) * tn, LANES), tn)],
                                      wf_ref.at[part], sem.at[part]) for part in range(2)]

    def cast():
        wgb_ref[...] = wf_ref[0].astype(BF16)
        wub_ref[...] = wf_ref[1].astype(BF16)

    _expert_weights(te_ref, nx_ref, copies, cast)
    half = x_ref.shape[1]

    @pl.when(i < nt_ref[0])
    def _():
        for rows in _row_blocks(o_ref.shape[0]):
            lo, hi = _unpack_bf16_pair(x_ref[rows, :])
            gate = _dot(lo, wgb_ref[:half, :]) + _dot(hi, wgb_ref[half:, :])
            up = _dot(lo, wub_ref[:half, :]) + _dot(hi, wub_ref[half:, :])
            o_ref[rows, :] = (_silu(gate) * up).astype(o_ref.dtype)

    @pl.when(i >= nt_ref[0])
    def _():
        o_ref[...] = jnp.zeros_like(o_ref)


def moe_up(xs, w_gu, tile_expert, next_expert, n_tiles, tn=512):
    rows, half = xs.shape
    d = 2 * half
    f = w_gu.shape[2] // 2
    tm, tn = MOE_ROW_TILE, _tile(f, tn)
    nj = f // tn
    grid_spec = pltpu.PrefetchScalarGridSpec(
        num_scalar_prefetch=3,
        grid=(nj, rows // tm),
        in_specs=[pl.BlockSpec((tm, half), lambda j, i, te, nx, nt: (i, 0)),
                  pl.BlockSpec(memory_space=pl.ANY)],
        out_specs=pl.BlockSpec((tm, tn), lambda j, i, te, nx, nt: (i, j)),
        scratch_shapes=[pltpu.VMEM((2, d, tn), F32), pltpu.VMEM((d, tn), BF16), pltpu.VMEM((d, tn), BF16),
                        pltpu.SemaphoreType.DMA((2,))],
    )
    return pl.pallas_call(
        _moe_up_kernel,
        grid_spec=grid_spec,
        out_shape=jax.ShapeDtypeStruct((rows, f), BF16),
        compiler_params=_params("arbitrary", "arbitrary"),
        name="moe_up",
    )(tile_expert, next_expert, n_tiles, xs, w_gu)


def _moe_down_kernel(te_ref, nx_ref, nt_ref, a_ref, w_ref, o_ref, wf_ref, wb_ref, sem):
    i = pl.program_id(1)
    tn = o_ref.shape[1]

    def copies(e, jj):
        return [pltpu.make_async_copy(w_ref.at[e, :, pl.ds(pl.multiple_of(jj * tn, LANES), tn)], wf_ref, sem)]

    def cast():
        wb_ref[...] = wf_ref[...].astype(BF16)

    _expert_weights(te_ref, nx_ref, copies, cast)

    @pl.when(i < nt_ref[0])
    def _():
        for rows in _row_blocks(o_ref.shape[0]):
            o_ref[rows, :] = _dot(a_ref[rows, :], wb_ref[...])

    @pl.when(i >= nt_ref[0])
    def _():
        o_ref[...] = jnp.zeros_like(o_ref)


def moe_down(a, w_down, tile_expert, next_expert, n_tiles, tn=1024):
    rows, f = a.shape
    d = w_down.shape[2]
    tm, tn = MOE_ROW_TILE, _tile(d, tn)
    grid_spec = pltpu.PrefetchScalarGridSpec(
        num_scalar_prefetch=3,
        grid=(d // tn, rows // tm),
        in_specs=[pl.BlockSpec((tm, f), lambda j, i, te, nx, nt: (i, 0)),
                  pl.BlockSpec(memory_space=pl.ANY)],
        out_specs=pl.BlockSpec((tm, tn), lambda j, i, te, nx, nt: (i, j)),
        scratch_shapes=[pltpu.VMEM((f, tn), F32), pltpu.VMEM((f, tn), BF16), pltpu.SemaphoreType.DMA(())],
    )
    return pl.pallas_call(
        _moe_down_kernel,
        grid_spec=grid_spec,
        out_shape=jax.ShapeDtypeStruct((rows, d), F32),
        compiler_params=_params("arbitrary", "arbitrary"),
        name="moe_down",
    )(tile_expert, next_expert, n_tiles, a, w_down)


def _combine_kernel(dest_ref, dest_next_ref, x_ref, w_ref, y_ref, g_ref, o_ref, xg_ref, rstd_ref, buf_ref, sem):
    i, n_steps = pl.program_id(0), pl.num_programs(0)
    tm = x_ref.shape[0]
    slot = lax.rem(i, 2)

    def gather(d_ref, s):
        def start(r, carry):
            for kk in range(TOP_K):
                _row_copy(y_ref, d_ref[0, 0, TOP_K * r + kk], buf_ref.at[s, kk], r, sem.at[s]).start()
            return carry

        lax.fori_loop(0, tm, start, 0, unroll=8)

    pl.when(i == 0)(lambda: gather(dest_ref, 0))
    for kk in range(TOP_K):
        pltpu.make_async_copy(y_ref.at[pl.ds(0, tm)], buf_ref.at[slot, kk], sem.at[slot]).wait()
    pl.when(i + 1 < n_steps)(lambda: gather(dest_next_ref, 1 - slot))
    w = w_ref[...]
    x_new = x_ref[...] + (w[:, 0:1] * buf_ref[slot, 0] + w[:, 1:2] * buf_ref[slot, 1])
    o_ref[...] = x_new
    xg_ref[...] = (x_new * g_ref[...]).astype(xg_ref.dtype)
    rstd_ref[...] = jnp.broadcast_to(lax.rsqrt(jnp.mean(x_new * x_new, axis=-1, keepdims=True) + EPS), rstd_ref.shape)


def moe_combine(x, y, dest, w, g_next, tm=256):
    n, d = x.shape
    tm = _tile(n, tm, SUBLANES)
    n_steps = n // tm
    dest3 = dest.reshape(n_steps, 1, TOP_K * tm)
    return pl.pallas_call(
        _combine_kernel,
        grid=(n_steps,),
        in_specs=[pl.BlockSpec((1, 1, TOP_K * tm), lambda i: (i, 0, 0), memory_space=pltpu.SMEM),
                  pl.BlockSpec((1, 1, TOP_K * tm), lambda i: (jnp.minimum(i + 1, n_steps - 1), 0, 0),
                               memory_space=pltpu.SMEM),
                  pl.BlockSpec((tm, d), lambda i: (i, 0)),
                  pl.BlockSpec((tm, LANES), lambda i: (i, 0)),
                  pl.BlockSpec(memory_space=pl.ANY),
                  pl.BlockSpec((1, d), lambda i: (0, 0))],
        out_specs=[pl.BlockSpec((tm, d), lambda i: (i, 0)),
                   pl.BlockSpec((tm, d), lambda i: (i, 0)),
                   pl.BlockSpec((tm, LANES), lambda i: (i, 0))],
        out_shape=[jax.ShapeDtypeStruct((n, d), F32),
                   jax.ShapeDtypeStruct((n, d), BF16),
                   jax.ShapeDtypeStruct((n, LANES), F32)],
        scratch_shapes=[pltpu.VMEM((2, TOP_K, tm, d), F32), pltpu.SemaphoreType.DMA((2,))],
        compiler_params=_params("arbitrary"),
        name="moe_combine",
    )(dest3, dest3, x, w, y, g_next.reshape(1, d))


def moe_layer(x, g, w_router, w_gu, w_down, g_next):
    n, d = x.shape
    n_exp = w_router.shape[1]
    tm = MOE_ROW_TILE
    h, gates, sel = moe_router(x, g, w_router)
    gates, sel = gates[:, :n_exp], sel[:, :n_exp]

    csum = jnp.cumsum(sel, axis=0)
    counts = csum[-1]
    group = ((counts + tm - 1) // tm) * tm
    ends = jnp.cumsum(group)
    slot = (ends - group)[None, :] + csum - 1
    big = jnp.iinfo(jnp.int32).max
    d_lo = jnp.min(jnp.where(sel > 0, slot, big), axis=1)
    d_hi = jnp.max(jnp.where(sel > 0, slot, -1), axis=1)
    w_lo = jnp.sum(jnp.where((sel > 0) & (slot == d_lo[:, None]), gates, 0.0), axis=1)
    w_hi = jnp.sum(jnp.where((sel > 0) & (slot == d_hi[:, None]), gates, 0.0), axis=1)
    dest = jnp.stack([d_lo, d_hi], axis=1).astype(jnp.int32)
    w = jnp.pad(jnp.stack([w_lo, w_hi], axis=1), ((0, 0), (0, LANES - TOP_K)))

    n_rows = TOP_K * n + n_exp * tm
    n_row_tiles = n_rows // tm
    tile_start = jnp.arange(n_row_tiles, dtype=jnp.int32) * tm
    tile_expert = jnp.minimum(jnp.sum((ends[None, :] <= tile_start[:, None]).astype(jnp.int32), axis=1), n_exp - 1)
    n_tiles = (ends[-1:] // tm).astype(jnp.int32)
    tile_idx = jnp.arange(n_row_tiles, dtype=jnp.int32)
    later_other = (tile_idx[None, :] > tile_idx[:, None]) & (tile_expert[None, :] != tile_expert[:, None])
    first_other = jnp.min(jnp.where(later_other, tile_idx[None, :], n_row_tiles), axis=1)
    expert_there = jnp.sum(jnp.where(tile_idx[None, :] == first_other[:, None], tile_expert[None, :], 0), axis=1)
    next_expert = jnp.where(first_other < n_row_tiles, expert_there, -1).astype(jnp.int32)

    xs = moe_dispatch(h, dest, ends.astype(jnp.int32), group.astype(jnp.int32), n_rows)
    a = moe_up(xs, w_gu, tile_expert, next_expert, n_tiles)
    y = moe_down(a, w_down, tile_expert, next_expert, n_tiles)
    return moe_combine(x, y, dest, w, g_next)


def kernel(x, p, mix_norm, ffn_norm, ple_norm, ple_w_gate, ple_w_proj, sg_w_in, sg_g_v, sg_w_s, sg_b_s, sg_w_out,
           ffn_w_gu, ffn_w_down, gla_w_in, gla_w_alpha, gla_b_alpha, gla_g_o, gla_w_out, moe_w_router, moe_w_gu,
           moe_w_down, final_norm):
    bsz, seq, d = x.shape
    n = bsz * seq
    depth = p.shape[0]
    x = x.reshape(n, d)
    p = p.reshape(depth, n, -1).astype(BF16)
    bf = lambda w: w.astype(BF16)
    ple_w_gate, ple_w_proj = bf(ple_w_gate), bf(ple_w_proj)

    xg, rstd = norm_inputs(x, mix_norm[0])
    for i in range(depth):
        j = i // 2
        g_after = None if i == depth - 1 else mix_norm[i + 1]
        if i % 2 == 0:
            z = matmul(xg, rstd, bf(sg_w_in[j]), sg_w_in.shape[2], act="gelu")
            gated = spatial_gate(z, sg_g_v[j], sg_w_s[j], sg_b_s[j])
            x, xg, rstd = matmul_residual(gated, bf(sg_w_out[j]), x, ffn_norm[i])
            a = swiglu_up(xg, rstd, bf(ffn_w_gu[j]))
            x, xg, rstd = matmul_residual(a, bf(ffn_w_down[j]), x, ple_norm[i], tn=1024, tk=3584)
        else:
            w_in = bf(gla_w_in[j])
            n_main = w_in.shape[1] - GLA_GATE_RANK
            qkvr = matmul(xg, rstd, w_in, n_main)
            a_low = gla_gate_proj(xg, rstd, w_in, n_main, GLA_GATE_RANK)
            og = gla(qkvr, a_low, gla_w_alpha[j], gla_b_alpha[j], gla_g_o[j], bsz, seq)
            x = matmul_residual(og, bf(gla_w_out[j]), x, tn=1024)
            x, xg, rstd = moe_layer(x, ffn_norm[i], moe_w_router[j], moe_w_gu[j], moe_w_down[j], ple_norm[i])
        out = ple(xg, rstd, ple_w_gate, ple_w_proj, i, p[i], x, g_after)
        x, xg, rstd = out if g_after is not None else (out, None, None)
    return rmsnorm(x, final_norm).reshape(bsz, seq, d)
```
